```python
import math
import jax, jax.numpy as jnp
from jax import lax
import numpy as np

D_MODEL = 2048
BATCH = 4
SEQ = 2048
DEPTH = 2

N_MIXERS = 2
N_CONV_LAYERS = (DEPTH + N_MIXERS - 1) // N_MIXERS
N_ATTN_LAYERS = DEPTH // N_MIXERS
CONV_WIDTH = 31
DIFF_HEADS = 8
DIFF_HEAD_DIM = D_MODEL // (2 * DIFF_HEADS)
DIFF_V_DIM = 2 * DIFF_HEAD_DIM
D_FF = 4 * D_MODEL
ROPE_THETA = 10000.0
Q_BLOCK = 128
DEEPNORM_ALPHA = (2.0 * DEPTH) ** 0.25
DEEPNORM_BETA = (8.0 * DEPTH) ** -0.25
LN_EPS = 1e-5
RMS_EPS = 1e-5
LAMBDA_PARAM_STD = 0.1
MAX_POS_OFFSET = 1024

kernel_name = "hybrid_conformer_conv_diff_attn_deepnorm_adaln"


def layer_norm(x, g, b):
    xf = x.astype(jnp.float32)
    mu = jnp.mean(xf, axis=-1, keepdims=True)
    var = jnp.mean(jnp.square(xf - mu), axis=-1, keepdims=True)
    return ((xf - mu) * lax.rsqrt(var + LN_EPS) * g.astype(jnp.float32) + b.astype(jnp.float32)).astype(x.dtype)


def apply_rope(t, cos, sin):
    tf = t.astype(jnp.float32)
    t1, t2 = jnp.split(tf, 2, axis=-1)
    return jnp.concatenate([t1 * cos - t2 * sin, t2 * cos + t1 * sin], axis=-1).astype(t.dtype)


def conformer_conv(h, w1, b1, w_dw, b_dw, g, b, w2, b2):
    u = h @ w1 + b1
    a, gt = jnp.split(u, 2, axis=-1)
    u = a * jax.nn.sigmoid(gt)
    u = lax.conv_general_dilated(
        u, w_dw[:, None, :], window_strides=(1,), padding=[(CONV_WIDTH - 1, 0)],
        dimension_numbers=('NWC', 'WIO', 'NWC'), feature_group_count=D_MODEL) + b_dw
    u = jax.nn.silu(layer_norm(u, g, b))
    return u @ w2 + b2


def diff_attention(h, positions, w_qkv, lq1, lk1, lq2, lk2, subln_g, w_o, lambda_init):
    B, T, _ = h.shape
    q, k, v = jnp.split(h @ w_qkv, 3, axis=-1)
    q = q.reshape(B, T, 2 * DIFF_HEADS, DIFF_HEAD_DIM)
    k = k.reshape(B, T, 2 * DIFF_HEADS, DIFF_HEAD_DIM)
    v = v.reshape(B, T, DIFF_HEADS, DIFF_V_DIM)
    inv_freq = ROPE_THETA ** (-jnp.arange(0, DIFF_HEAD_DIM, 2, dtype=jnp.float32) / DIFF_HEAD_DIM)
    ang = positions.astype(jnp.float32)[..., None] * inv_freq
    cos = jnp.cos(ang)[:, :, None, :]
    sin = jnp.sin(ang)[:, :, None, :]
    q = apply_rope(q, cos, sin) * (DIFF_HEAD_DIM ** -0.5)
    k = apply_rope(k, cos, sin)
    f32 = jnp.float32
    lam = (jnp.exp(jnp.sum(lq1.astype(f32) * lk1.astype(f32)))
           - jnp.exp(jnp.sum(lq2.astype(f32) * lk2.astype(f32))) + lambda_init)
    outs = []
    for i in range(T // Q_BLOCK):
        q0 = i * Q_BLOCK
        kend = q0 + Q_BLOCK
        qb = q[:, q0:kend]
        kb = k[:, :kend]
        vb = v[:, :kend]
        s = jnp.einsum('bqhd,bkhd->bhqk', qb, kb, preferred_element_type=jnp.float32)
        mask = (q0 + jnp.arange(Q_BLOCK))[:, None] >= jnp.arange(kend)[None, :]
        s = jnp.where(mask, s, -jnp.inf)
        p = jax.nn.softmax(s, axis=-1).reshape(B, DIFF_HEADS, 2, Q_BLOCK, kend)
        a = p[:, :, 0] - lam * p[:, :, 1]
        outs.append(jnp.einsum('bhqk,bkhe->bqhe', a.astype(vb.dtype), vb))
    o = jnp.concatenate(outs, axis=1).astype(jnp.float32)
    o = o * lax.rsqrt(jnp.mean(jnp.square(o), axis=-1, keepdims=True) + RMS_EPS)
    o = o * subln_g.astype(jnp.float32) * (1.0 - lambda_init)
    return o.astype(h.dtype).reshape(B, T, D_MODEL) @ w_o


def squared_relu_mlp(h, w1, b1, w2, b2):
    u = jnp.square(jax.nn.relu(h @ w1 + b1))
    return u @ w2 + b2


def setup_inputs(seed: int = 0) -> dict:
    key = jax.random.key(seed)
    ks = iter(jax.random.split(key, 40))
    f32 = jnp.float32
    D = D_MODEL

    def nrm(shape, scale):
        return jax.random.normal(next(ks), shape, f32) * scale

    def gain(shape):
        return 1.0 + nrm(shape, 0.02)

    x = jax.random.normal(next(ks), (BATCH, SEQ, D), f32)
    c = jax.random.normal(next(ks), (BATCH, D), f32)
    positions = (jnp.arange(SEQ, dtype=jnp.int32)[None, :]
                 + jax.random.randint(next(ks), (BATCH, 1), 0, MAX_POS_OFFSET, dtype=jnp.int32))
    ada_w = nrm((DEPTH, D, 6 * D), D ** -0.5)
    ada_b = nrm((DEPTH, 6 * D), 0.01)
    ln_mix_g = gain((DEPTH, D)); ln_mix_b = nrm((DEPTH, D), 0.01)
    ln_ffn_g = gain((DEPTH, D)); ln_ffn_b = nrm((DEPTH, D), 0.01)
    conv_pw1_w = nrm((N_CONV_LAYERS, D, 2 * D), D ** -0.5)
    conv_pw1_b = nrm((N_CONV_LAYERS, 2 * D), 0.01)
    conv_dw_w = nrm((N_CONV_LAYERS, CONV_WIDTH, D), CONV_WIDTH ** -0.5)
    conv_dw_b = nrm((N_CONV_LAYERS, D), 0.01)
    conv_ln_g = gain((N_CONV_LAYERS, D)); conv_ln_b = nrm((N_CONV_LAYERS, D), 0.01)
    conv_pw2_w = nrm((N_CONV_LAYERS, D, D), D ** -0.5 * DEEPNORM_BETA)
    conv_pw2_b = nrm((N_CONV_LAYERS, D), 0.01)
    attn_qkv_w = jnp.concatenate([
        nrm((N_ATTN_LAYERS, D, 2 * D), D ** -0.5),
        nrm((N_ATTN_LAYERS, D, D), D ** -0.5 * DEEPNORM_BETA)],
        axis=-1)
    attn_lq1 = nrm((N_ATTN_LAYERS, DIFF_HEAD_DIM), LAMBDA_PARAM_STD)
    attn_lk1 = nrm((N_ATTN_LAYERS, DIFF_HEAD_DIM), LAMBDA_PARAM_STD)
    attn_lq2 = nrm((N_ATTN_LAYERS, DIFF_HEAD_DIM), LAMBDA_PARAM_STD)
    attn_lk2 = nrm((N_ATTN_LAYERS, DIFF_HEAD_DIM), LAMBDA_PARAM_STD)
    attn_subln_g = gain((N_ATTN_LAYERS, DIFF_V_DIM))
    attn_o_w = nrm((N_ATTN_LAYERS, D, D), D ** -0.5 * DEEPNORM_BETA)
    mlp_w1 = nrm((DEPTH, D, D_FF), D ** -0.5)
    mlp_b1 = nrm((DEPTH, D_FF), 0.01)
    mlp_w2 = nrm((DEPTH, D_FF, D), D_FF ** -0.5 * DEEPNORM_BETA)
    mlp_b2 = nrm((DEPTH, D), 0.01)
    return {"x": x, "c": c, "positions": positions,
            "ada_w": ada_w, "ada_b": ada_b,
            "ln_mix_g": ln_mix_g, "ln_mix_b": ln_mix_b, "ln_ffn_g": ln_ffn_g, "ln_ffn_b": ln_ffn_b,
            "conv_pw1_w": conv_pw1_w, "conv_pw1_b": conv_pw1_b, "conv_dw_w": conv_dw_w,
            "conv_dw_b": conv_dw_b, "conv_ln_g": conv_ln_g, "conv_ln_b": conv_ln_b,
            "conv_pw2_w": conv_pw2_w, "conv_pw2_b": conv_pw2_b,
            "attn_qkv_w": attn_qkv_w, "attn_lq1": attn_lq1, "attn_lk1": attn_lk1,
            "attn_lq2": attn_lq2, "attn_lk2": attn_lk2, "attn_subln_g": attn_subln_g,
            "attn_o_w": attn_o_w,
            "mlp_w1": mlp_w1, "mlp_b1": mlp_b1, "mlp_w2": mlp_w2, "mlp_b2": mlp_b2}


def reference(x, c, positions, ada_w, ada_b, ln_mix_g, ln_mix_b, ln_ffn_g, ln_ffn_b,
              conv_pw1_w, conv_pw1_b, conv_dw_w, conv_dw_b, conv_ln_g, conv_ln_b,
              conv_pw2_w, conv_pw2_b, attn_qkv_w, attn_lq1, attn_lk1, attn_lq2, attn_lk2,
              attn_subln_g, attn_o_w, mlp_w1, mlp_b1, mlp_w2, mlp_b2):
    cond = jax.nn.silu(c)
    for i in range(DEPTH):
        mod = cond @ ada_w[i] + ada_b[i]
        sh_m, sc_m, g_m, sh_f, sc_f, g_f = [m[:, None, :] for m in jnp.split(mod, 6, axis=-1)]
        h = x * (1.0 + sc_m) + sh_m
        j = i // N_MIXERS
        if i % N_MIXERS == 0:
            y = conformer_conv(h, conv_pw1_w[j], conv_pw1_b[j], conv_dw_w[j], conv_dw_b[j],
                               conv_ln_g[j], conv_ln_b[j], conv_pw2_w[j], conv_pw2_b[j])
        else:
            lambda_init = 0.8 - 0.6 * math.exp(-0.3 * i)
            y = diff_attention(h, positions, attn_qkv_w[j], attn_lq1[j], attn_lk1[j],
                               attn_lq2[j], attn_lk2[j], attn_subln_g[j], attn_o_w[j],
                               lambda_init)
        x = layer_norm(DEEPNORM_ALPHA * x + g_m * y, ln_mix_g[i], ln_mix_b[i])
        h = x * (1.0 + sc_f) + sh_f
        y = squared_relu_mlp(h, mlp_w1[i], mlp_b1[i], mlp_w2[i], mlp_b2[i])
        x = layer_norm(DEEPNORM_ALPHA * x + g_f * y, ln_ffn_g[i], ln_ffn_b[i])
    return x
```

```python
import functools
import math

import jax
import jax.numpy as jnp
from jax import lax
from jax.experimental import pallas as pl
from jax.experimental.pallas import tpu as pltpu

F32 = jnp.float32
BF16 = jnp.bfloat16

D_MODEL = 2048
DEPTH = 2
N_MIXERS = 2
CONV_WIDTH = 31
DIFF_HEADS = 8
DIFF_HEAD_DIM = D_MODEL // (2 * DIFF_HEADS)
DIFF_V_DIM = 2 * DIFF_HEAD_DIM
D_FF = 4 * D_MODEL
ROPE_THETA = 10000.0
DEEPNORM_ALPHA = (2.0 * DEPTH) ** 0.25
LN_EPS = 1e-5
RMS_EPS = 1e-5

LANES = 128
CONV_HALO = 32
VMEM_LIMIT = 60 * 1024 * 1024


def _dot(a, b):
    return jnp.dot(a, b, preferred_element_type=F32)


def _layer_norm(z, g, b):
    mu = jnp.mean(z, axis=-1, keepdims=True)
    zc = z - mu
    var = jnp.mean(zc * zc, axis=-1, keepdims=True)
    return zc * lax.rsqrt(var + LN_EPS) * g + b


def _cast_rows(src_ref, dst_ref, rows_per_step=256):
    n = src_ref.shape[0] // rows_per_step

    def body(i, carry):
        r = pl.multiple_of(i * rows_per_step, rows_per_step)
        dst_ref[pl.ds(r, rows_per_step), :] = src_ref[pl.ds(r, rows_per_step), :].astype(dst_ref.dtype)
        return carry

    lax.fori_loop(0, n, body, 0)


def _params(*sem):
    return pltpu.CompilerParams(dimension_semantics=sem, vmem_limit_bytes=VMEM_LIMIT)


def _adaln_kernel(c_ref, w_ref, b_ref, o_ref):
    c = c_ref[...]
    cond = (c * jax.nn.sigmoid(c)).astype(BF16)
    o_ref[...] = _dot(cond, w_ref[...].astype(BF16)) + b_ref[...]


def _adaln(c, ada_w, ada_b, tn=1024):
    depth, d, n = ada_w.shape
    bsz = c.shape[0]
    return pl.pallas_call(
        _adaln_kernel,
        grid=(depth, n // tn),
        in_specs=[
            pl.BlockSpec((bsz, d), lambda i, j: (0, 0)),
            pl.BlockSpec((None, d, tn), lambda i, j: (i, 0, j)),
            pl.BlockSpec((None, 1, tn), lambda i, j: (i, 0, j)),
        ],
        out_specs=pl.BlockSpec((None, bsz, tn), lambda i, j: (i, 0, j)),
        out_shape=jax.ShapeDtypeStruct((depth, bsz, n), F32),
        compiler_params=_params("arbitrary", "arbitrary"),
        name="adaln",
    )(c, ada_w, ada_b.reshape(depth, 1, n))


def _pw1_glu_kernel(x_ref, sc_ref, sh_ref, wa_ref, wg_ref, ba_ref, bg_ref, o_ref, wa_bf, wg_bf):
    @pl.when(pl.program_id(1) == 0)
    def _():
        _cast_rows(wa_ref, wa_bf)
        _cast_rows(wg_ref, wg_bf)

    h = (x_ref[...] * (1.0 + sc_ref[0]) + sh_ref[0]).astype(BF16)
    a = _dot(h, wa_bf[...]) + ba_ref[...]
    g = _dot(h, wg_bf[...]) + bg_ref[...]
    o_ref[...] = a * jax.nn.sigmoid(g)


def _pw1_glu(x, sc, sh, w, b, layer, seq, tm=512, tn=512):
    m, d = x.shape
    n_half = w.shape[2] // 2
    gate_off = n_half // tn
    b3 = b.reshape(b.shape[0], 1, b.shape[1])
    return pl.pallas_call(
        _pw1_glu_kernel,
        grid=(n_half // tn, m // tm),
        in_specs=[
            pl.BlockSpec((tm, d), lambda j, i: (i, 0)),
            pl.BlockSpec((1, 1, d), lambda j, i: (i * tm // seq, 0, 0)),
            pl.BlockSpec((1, 1, d), lambda j, i: (i * tm // seq, 0, 0)),
            pl.BlockSpec((None, d, tn), lambda j, i: (layer, 0, j)),
            pl.BlockSpec((None, d, tn), lambda j, i: (layer, 0, j + gate_off)),
            pl.BlockSpec((None, 1, tn), lambda j, i: (layer, 0, j)),
            pl.BlockSpec((None, 1, tn), lambda j, i: (layer, 0, j + gate_off)),
        ],
        out_specs=pl.BlockSpec((tm, tn), lambda j, i: (i, j)),
        out_shape=jax.ShapeDtypeStruct((m, n_half), F32),
        scratch_shapes=[pltpu.VMEM((d, tn), BF16), pltpu.VMEM((d, tn), BF16)],
        compiler_params=_params("arbitrary", "arbitrary"),
        name="pw1_glu",
    )(x, sc, sh, w, w, b3, b3)


def _conv_ln_silu_kernel(cur_ref, halo_ref, w_ref, bdw_ref, g_ref, b_ref, o_ref, buf, y, *, tiles_per_seq,
                         row_chunk):
    tt, d = cur_ref.shape
    groups = d // LANES
    first = (pl.program_id(0) % tiles_per_seq) == 0
    for c in range(groups):
        lanes = slice(c * LANES, (c + 1) * LANES)
        buf[c, CONV_HALO:, :] = cur_ref[:, lanes]
        buf[c, :CONV_HALO, :] = jnp.where(first, 0.0, halo_ref[:, lanes])

    tap0 = CONV_HALO - (CONV_WIDTH - 1)

    def group_body(c, carry):
        for r in range(tt // row_chunk):
            acc = jnp.zeros((row_chunk, LANES), F32)
            for k in range(CONV_WIDTH):
                acc = acc + buf[c, pl.ds(r * row_chunk + tap0 + k, row_chunk), :] * w_ref[c, pl.ds(k, 1), :]
            y[c, pl.ds(r * row_chunk, row_chunk), :] = acc + bdw_ref[c]
        return carry

    lax.fori_loop(0, groups, group_body, 0)

    total = jnp.zeros((tt, LANES), F32)
    for c in range(groups):
        total = total + y[c]
    mu = jnp.sum(total, axis=-1, keepdims=True) * (1.0 / d)
    sq = jnp.zeros((tt, LANES), F32)
    for c in range(groups):
        yc = y[c] - mu
        sq = sq + yc * yc
    var = jnp.sum(sq, axis=-1, keepdims=True) * (1.0 / d)
    rstd = lax.rsqrt(var + LN_EPS)
    for c in range(groups):
        lanes = slice(c * LANES, (c + 1) * LANES)
        z = (y[c] - mu) * rstd * g_ref[:, lanes] + b_ref[:, lanes]
        o_ref[:, lanes] = (z * jax.nn.sigmoid(z)).astype(o_ref.dtype)


def _conv_ln_silu(u, w_dw, b_dw, ln_g, ln_b, seq, tt=256, row_chunk=64):
    m, d = u.shape
    groups = d // LANES
    halo_per_tile = tt // CONV_HALO
    w_g = jnp.pad(w_dw, ((0, CONV_HALO - CONV_WIDTH), (0, 0))).reshape(CONV_HALO, groups, LANES).transpose(1, 0, 2)
    b_g = b_dw.reshape(groups, 1, LANES)
    kern = functools.partial(_conv_ln_silu_kernel, tiles_per_seq=seq // tt, row_chunk=row_chunk)
    return pl.pallas_call(
        kern,
        grid=(m // tt,),
        in_specs=[
            pl.BlockSpec((tt, d), lambda i: (i, 0)),
            pl.BlockSpec((CONV_HALO, d), lambda i: (jnp.maximum(i * halo_per_tile - 1, 0), 0)),
            pl.BlockSpec((groups, CONV_HALO, LANES), lambda i: (0, 0, 0)),
            pl.BlockSpec((groups, 1, LANES), lambda i: (0, 0, 0)),
            pl.BlockSpec((1, d), lambda i: (0, 0)),
            pl.BlockSpec((1, d), lambda i: (0, 0)),
        ],
        out_specs=pl.BlockSpec((tt, d), lambda i: (i, 0)),
        out_shape=jax.ShapeDtypeStruct((m, d), BF16),
        scratch_shapes=[pltpu.VMEM((groups, CONV_HALO + tt, LANES), F32), pltpu.VMEM((groups, tt, LANES), F32)],
        compiler_params=_params("arbitrary"),
        name="conv_ln_silu",
    )(u, u, w_g, b_g, ln_g.reshape(1, d), ln_b.reshape(1, d))


def _proj_res_ln_kernel(*refs, has_bias):
    if has_bias:
        a_ref, w_ref, bias_ref, x_ref, gate_ref, lng_ref, lnb_ref, o_ref, w_bf = refs
    else:
        a_ref, w_ref, x_ref, gate_ref, lng_ref, lnb_ref, o_ref, w_bf = refs

    @pl.when(pl.program_id(0) == 0)
    def _():
        _cast_rows(w_ref, w_bf)

    y = _dot(a_ref[...], w_bf[...])
    if has_bias:
        y = y + bias_ref[...]
    z = DEEPNORM_ALPHA * x_ref[...] + gate_ref[0] * y
    o_ref[...] = _layer_norm(z, lng_ref[...], lnb_ref[...])


def _proj_res_ln(a, w, bias, x, gate, ln_g, ln_b, w_layer, ln_layer, seq, tm=512):
    m, k = a.shape
    d = w.shape[2]
    in_specs = [
        pl.BlockSpec((tm, k), lambda i: (i, 0)),
        pl.BlockSpec((None, k, d), lambda i: (w_layer, 0, 0), pipeline_mode=pl.Buffered(1)),
    ]
    args = [a, w]
    if bias is not None:
        in_specs.append(pl.BlockSpec((None, 1, d), lambda i: (w_layer, 0, 0)))
        args.append(bias.reshape(bias.shape[0], 1, d))
    in_specs += [
        pl.BlockSpec((tm, d), lambda i: (i, 0)),
        pl.BlockSpec((1, 1, d), lambda i: (i * tm // seq, 0, 0)),
        pl.BlockSpec((None, 1, d), lambda i: (ln_layer, 0, 0)),
        pl.BlockSpec((None, 1, d), lambda i: (ln_layer, 0, 0)),
    ]
    args += [x, gate, ln_g.reshape(-1, 1, d), ln_b.reshape(-1, 1, d)]
    return pl.pallas_call(
        functools.partial(_proj_res_ln_kernel, has_bias=bias is not None),
        grid=(m // tm,),
        in_specs=in_specs,
        out_specs=pl.BlockSpec((tm, d), lambda i: (i, 0)),
        out_shape=jax.ShapeDtypeStruct((m, d), F32),
        scratch_shapes=[pltpu.VMEM((k, d), BF16)],
        compiler_params=_params("arbitrary"),
        name="proj_res_ln",
    )(*args)


def _mlp_kernel(x_ref, sc_ref, sh_ref, gate_ref, w1_ref, b1_ref, w2_ref, b2_ref, lng_ref, lnb_ref, o_ref, h_bf):
    f = pl.program_id(1)

    @pl.when(f == 0)
    def _():
        h_bf[...] = (x_ref[...] * (1.0 + sc_ref[0]) + sh_ref[0]).astype(BF16)
        o_ref[...] = jnp.zeros_like(o_ref)

    u = _dot(h_bf[...], w1_ref[...].astype(BF16)) + b1_ref[...]
    u = jnp.square(jnp.maximum(u, 0.0)).astype(BF16)
    o_ref[...] += _dot(u, w2_ref[...].astype(BF16))

    @pl.when(f == pl.num_programs(1) - 1)
    def _():
        y = o_ref[...] + b2_ref[...]
        z = DEEPNORM_ALPHA * x_ref[...] + gate_ref[0] * y
        o_ref[...] = _layer_norm(z, lng_ref[...], lnb_ref[...])


def _mlp(x, sc, sh, gate, w1, b1, w2, b2, ln_g, ln_b, layer, seq, tm=1024, tf=512):
    m, d = x.shape
    dff = w1.shape[2]
    mod_spec = pl.BlockSpec((1, 1, d), lambda i, f: (i * tm // seq, 0, 0))
    vec_spec = pl.BlockSpec((None, 1, d), lambda i, f: (layer, 0, 0))
    return pl.pallas_call(
        _mlp_kernel,
        grid=(m // tm, dff // tf),
        in_specs=[
            pl.BlockSpec((tm, d), lambda i, f: (i, 0), pipeline_mode=pl.Buffered(1)),
            mod_spec, mod_spec, mod_spec,
            pl.BlockSpec((None, d, tf), lambda i, f: (layer, 0, f)),
            pl.BlockSpec((None, 1, tf), lambda i, f: (layer, 0, f)),
            pl.BlockSpec((None, tf, d), lambda i, f: (layer, f, 0)),
            vec_spec, vec_spec, vec_spec,
        ],
        out_specs=pl.BlockSpec((tm, d), lambda i, f: (i, 0)),
        out_shape=jax.ShapeDtypeStruct((m, d), F32),
        scratch_shapes=[pltpu.VMEM((tm, d), BF16)],
        compiler_params=_params("arbitrary", "arbitrary"),
        name="mlp",
    )(x, sc, sh, gate, w1, b1.reshape(-1, 1, dff), w2, b2.reshape(-1, 1, d),
      ln_g.reshape(-1, 1, d), ln_b.reshape(-1, 1, d))


def _rope_table_kernel(pos_ref, inv_freq_ref, sign_ref, cos_ref, sin_ref):
    ang = pos_ref[...].astype(F32) * inv_freq_ref[...]
    cos_ref[...] = jnp.cos(ang)
    sin_ref[...] = jnp.sin(ang) * sign_ref[...]


def _rope_tables(positions, tm=1024):
    m = positions.size
    half = DIFF_HEAD_DIM // 2
    inv_freq = ROPE_THETA ** (-jnp.arange(0, DIFF_HEAD_DIM, 2, dtype=F32) / DIFF_HEAD_DIM)
    inv_freq = jnp.concatenate([inv_freq, inv_freq]).reshape(1, DIFF_HEAD_DIM)
    sign = jnp.concatenate([-jnp.ones((half,), F32), jnp.ones((half,), F32)]).reshape(1, DIFF_HEAD_DIM)
    pos = jnp.broadcast_to(positions.reshape(m, 1), (m, DIFF_HEAD_DIM))
    tab = jax.ShapeDtypeStruct((m, DIFF_HEAD_DIM), F32)
    return pl.pallas_call(
        _rope_table_kernel,
        grid=(m // tm,),
        in_specs=[
            pl.BlockSpec((tm, DIFF_HEAD_DIM), lambda i: (i, 0)),
            pl.BlockSpec((1, DIFF_HEAD_DIM), lambda i: (0, 0)),
            pl.BlockSpec((1, DIFF_HEAD_DIM), lambda i: (0, 0)),
        ],
        out_specs=[pl.BlockSpec((tm, DIFF_HEAD_DIM), lambda i: (i, 0))] * 2,
        out_shape=[tab, tab],
        compiler_params=_params("arbitrary"),
        name="rope_tables",
    )(pos, inv_freq, sign)


def _qkv_rope_kernel(x_ref, sc_ref, sh_ref, w_ref, cos_ref, sin_ref, o_ref, w_bf, *, q_tiles, rope_tiles):
    j = pl.program_id(0)

    @pl.when(pl.program_id(1) == 0)
    def _():
        _cast_rows(w_ref, w_bf)

    h = (x_ref[...] * (1.0 + sc_ref[0]) + sh_ref[0]).astype(BF16)
    y = _dot(h, w_bf[...])
    tn = y.shape[1]

    @pl.when(j < rope_tiles)
    def _():
        scale = jnp.where(j < q_tiles, DIFF_HEAD_DIM ** -0.5, 1.0)
        cos = cos_ref[...]
        sin = sin_ref[...]
        for c in range(tn // DIFF_HEAD_DIM):
            lanes = slice(c * DIFF_HEAD_DIM, (c + 1) * DIFF_HEAD_DIM)
            t = y[:, lanes]
            r = t * cos + pltpu.roll(t, DIFF_HEAD_DIM // 2, 1) * sin
            o_ref[:, lanes] = (r * scale).astype(o_ref.dtype)

    @pl.when(j >= rope_tiles)
    def _():
        o_ref[...] = y.astype(o_ref.dtype)


def _qkv_rope(x, sc, sh, w, cos, sin, layer, seq, tm=512, tn=1024):
    m, d = x.shape
    n = w.shape[2]
    kern = functools.partial(_qkv_rope_kernel, q_tiles=D_MODEL // tn, rope_tiles=2 * D_MODEL // tn)
    return pl.pallas_call(
        kern,
        grid=(n // tn, m // tm),
        in_specs=[
            pl.BlockSpec((tm, d), lambda j, i: (i, 0)),
            pl.BlockSpec((1, 1, d), lambda j, i: (i * tm // seq, 0, 0)),
            pl.BlockSpec((1, 1, d), lambda j, i: (i * tm // seq, 0, 0)),
            pl.BlockSpec((None, d, tn), lambda j, i: (layer, 0, j)),
            pl.BlockSpec((tm, DIFF_HEAD_DIM), lambda j, i: (i, 0)),
            pl.BlockSpec((tm, DIFF_HEAD_DIM), lambda j, i: (i, 0)),
        ],
        out_specs=pl.BlockSpec((tm, tn), lambda j, i: (i, j)),
        out_shape=jax.ShapeDtypeStruct((m, n), BF16),
        scratch_shapes=[pltpu.VMEM((d, tn), BF16)],
        compiler_params=_params("arbitrary", "arbitrary"),
        name="qkv_rope",
    )(x, sc, sh, w, cos, sin)


def _diff_attn_kernel(q_ref, k_ref, v_ref, lq1_ref, lk1_ref, lq2_ref, lk2_ref, g_ref, o_ref,
                      m_sc, l_sc, acc_sc, *, lambda_init):
    tq = q_ref.shape[0]
    dh = DIFF_HEAD_DIM
    i = pl.program_id(2)

    m_sc[...] = jnp.full_like(m_sc, -jnp.inf)
    l_sc[...] = jnp.zeros_like(l_sc)
    acc_sc[...] = jnp.zeros_like(acc_sc)

    def chunk(kc, masked):
        k0 = pl.multiple_of(kc * tq, tq)
        v = v_ref[pl.ds(k0, tq), :]
        for mp in range(2):
            q = q_ref[:, mp * dh:(mp + 1) * dh]
            k = k_ref[pl.ds(k0, tq), mp * dh:(mp + 1) * dh]
            s = lax.dot_general(q, k, (((1,), (1,)), ((), ())), preferred_element_type=F32)
            if masked:
                row = lax.broadcasted_iota(jnp.int32, s.shape, 0)
                col = lax.broadcasted_iota(jnp.int32, s.shape, 1)
                s = jnp.where(row >= col, s, -jnp.inf)
            m_old = m_sc[mp]
            m_new = jnp.maximum(m_old, jnp.max(s, axis=-1, keepdims=True))
            alpha = jnp.exp(m_old - m_new)
            p = jnp.exp(s - m_new)
            l_sc[mp] = alpha * l_sc[mp] + jnp.sum(p, axis=-1, keepdims=True)
            acc_sc[mp] = alpha * acc_sc[mp] + _dot(p.astype(v.dtype), v)
            m_sc[mp] = m_new

    def body(kc, carry):
        chunk(kc, False)
        return carry

    lax.fori_loop(0, i, body, 0)
    chunk(i, True)

    lam = (jnp.exp(jnp.sum(lq1_ref[...] * lk1_ref[...], axis=-1, keepdims=True))
           - jnp.exp(jnp.sum(lq2_ref[...] * lk2_ref[...], axis=-1, keepdims=True)) + lambda_init)
    o = acc_sc[0] / l_sc[0] - lam * (acc_sc[1] / l_sc[1])
    o = o * lax.rsqrt(jnp.mean(o * o, axis=-1, keepdims=True) + RMS_EPS)
    o = o * g_ref[...] * (1.0 - lambda_init)
    o_ref[...] = o.astype(o_ref.dtype)


def _diff_attn(qkv, lq1, lk1, lq2, lk2, subln_g, layer, bsz, seq, lambda_init, tq=256):
    m = qkv.shape[0]
    nq = seq // tq
    vd = DIFF_V_DIM
    lam_spec = pl.BlockSpec((None, 1, DIFF_HEAD_DIM), lambda b, h, i: (layer, 0, 0))
    lam_args = [a.reshape(-1, 1, DIFF_HEAD_DIM) for a in (lq1, lk1, lq2, lk2)]
    return pl.pallas_call(
        functools.partial(_diff_attn_kernel, lambda_init=lambda_init),
        grid=(bsz, DIFF_HEADS, nq),
        in_specs=[
            pl.BlockSpec((tq, vd), lambda b, h, i: (b * nq + i, h)),
            pl.BlockSpec((seq, vd), lambda b, h, i: (b, DIFF_HEADS + h)),
            pl.BlockSpec((seq, vd), lambda b, h, i: (b, 2 * DIFF_HEADS + h)),
            lam_spec, lam_spec, lam_spec, lam_spec,
            pl.BlockSpec((None, 1, vd), lambda b, h, i: (layer, 0, 0)),
        ],
        out_specs=pl.BlockSpec((tq, vd), lambda b, h, i: (b * nq + i, h)),
        out_shape=jax.ShapeDtypeStruct((m, DIFF_HEADS * vd), BF16),
        scratch_shapes=[pltpu.VMEM((2, tq, 1), F32), pltpu.VMEM((2, tq, 1), F32), pltpu.VMEM((2, tq, vd), F32)],
        compiler_params=_params("arbitrary", "arbitrary", "arbitrary"),
        name="diff_attn",
    )(qkv, qkv, qkv, *lam_args, subln_g.reshape(-1, 1, vd))


def kernel(x, c, positions, ada_w, ada_b, ln_mix_g, ln_mix_b, ln_ffn_g, ln_ffn_b, conv_pw1_w, conv_pw1_b, conv_dw_w, conv_dw_b, conv_ln_g, conv_ln_b, conv_pw2_w, conv_pw2_b, attn_qkv_w, attn_lq1, attn_lk1, attn_lq2, attn_lk2, attn_subln_g, attn_o_w, mlp_w1, mlp_b1, mlp_w2, mlp_b2):
    bsz, seq, d = x.shape
    xf = x.reshape(bsz * seq, d)
    mod = _adaln(c, ada_w, ada_b).reshape(DEPTH, bsz, 6, 1, d)
    cos, sin = _rope_tables(positions)
    for i in range(DEPTH):
        sh_m, sc_m, g_m, sh_f, sc_f, g_f = [mod[i, :, k] for k in range(6)]
        j = i // N_MIXERS
        if i % N_MIXERS == 0:
            u = _pw1_glu(xf, sc_m, sh_m, conv_pw1_w, conv_pw1_b, j, seq)
            v = _conv_ln_silu(u, conv_dw_w[j], conv_dw_b[j], conv_ln_g[j], conv_ln_b[j], seq)
            xf = _proj_res_ln(v, conv_pw2_w, conv_pw2_b, xf, g_m, ln_mix_g, ln_mix_b, j, i, seq)
        else:
            lambda_init = 0.8 - 0.6 * math.exp(-0.3 * i)
            qkv = _qkv_rope(xf, sc_m, sh_m, attn_qkv_w, cos, sin, j, seq)
            o = _diff_attn(qkv, attn_lq1, attn_lk1, attn_lq2, attn_lk2, attn_subln_g, j, bsz, seq, lambda_init)
            xf = _proj_res_ln(o, attn_o_w, None, xf, g_m, ln_mix_g, ln_mix_b, j, i, seq)
        xf = _mlp(xf, sc_f, sh_f, g_f, mlp_w1, mlp_b1, mlp_w2, mlp_b2, ln_ffn_g, ln_ffn_b, i, seq)
    return xf.reshape(bsz, seq, d)
```

```python
import functools
import math

import jax
import jax.numpy as jnp
from jax import lax
from jax.experimental import pallas as pl
from jax.experimental.pallas import tpu as pltpu

F32 = jnp.float32
BF16 = jnp.bfloat16

D_MODEL = 2048
DEPTH = 2
N_MIXERS = 2
CONV_WIDTH = 31
DIFF_HEADS = 8
DIFF_HEAD_DIM = D_MODEL // (2 * DIFF_HEADS)
DIFF_V_DIM = 2 * DIFF_HEAD_DIM
D_FF = 4 * D_MODEL
ROPE_THETA = 10000.0
DEEPNORM_ALPHA = (2.0 * DEPTH) ** 0.25
LN_EPS = 1e-5
RMS_EPS = 1e-5

LANES = 128
CONV_HALO = 32
VMEM_LIMIT = 60 * 1024 * 1024


def _dot(a, b):
    return jnp.dot(a, b, preferred_element_type=F32)


def _layer_norm(z, g, b):
    mu = jnp.mean(z, axis=-1, keepdims=True)
    zc = z - mu
    var = jnp.mean(zc * zc, axis=-1, keepdims=True)
    return zc * lax.rsqrt(var + LN_EPS) * g + b


def _cast_rows(src_ref, dst_ref, rows_per_step=256):
    n = src_ref.shape[0] // rows_per_step

    def body(i, carry):
        r = pl.multiple_of(i * rows_per_step, rows_per_step)
        dst_ref[pl.ds(r, rows_per_step), :] = src_ref[pl.ds(r, rows_per_step), :].astype(dst_ref.dtype)
        return carry

    lax.fori_loop(0, n, body, 0)


def _params(*sem):
    return pltpu.CompilerParams(dimension_semantics=sem, vmem_limit_bytes=VMEM_LIMIT)


def _adaln_kernel(c_ref, w_ref, b_ref, o_ref):
    c = c_ref[...]
    cond = (c * jax.nn.sigmoid(c)).astype(BF16)
    o_ref[...] = _dot(cond, w_ref[...].astype(BF16)) + b_ref[...]


def _adaln(c, ada_w, ada_b, tn=1024):
    depth, d, n = ada_w.shape
    bsz = c.shape[0]
    return pl.pallas_call(
        _adaln_kernel,
        grid=(depth, n // tn),
        in_specs=[
            pl.BlockSpec((bsz, d), lambda i, j: (0, 0)),
            pl.BlockSpec((None, d, tn), lambda i, j: (i, 0, j)),
            pl.BlockSpec((None, 1, tn), lambda i, j: (i, 0, j)),
        ],
        out_specs=pl.BlockSpec((None, bsz, tn), lambda i, j: (i, 0, j)),
        out_shape=jax.ShapeDtypeStruct((depth, bsz, n), F32),
        compiler_params=_params("arbitrary", "arbitrary"),
        name="adaln",
    )(c, ada_w, ada_b.reshape(depth, 1, n))


def _pw1_glu_kernel(x_ref, sc_ref, sh_ref, wa_ref, wg_ref, ba_ref, bg_ref, o_ref, wa_bf, wg_bf):
    @pl.when(pl.program_id(1) == 0)
    def _():
        _cast_rows(wa_ref, wa_bf)
        _cast_rows(wg_ref, wg_bf)

    h = (x_ref[...] * (1.0 + sc_ref[0]) + sh_ref[0]).astype(BF16)
    a = _dot(h, wa_bf[...]) + ba_ref[...]
    g = _dot(h, wg_bf[...]) + bg_ref[...]
    o_ref[...] = a * jax.nn.sigmoid(g)


def _pw1_glu(x, sc, sh, w, b, layer, seq, tm=512, tn=512):
    m, d = x.shape
    n_half = w.shape[2] // 2
    gate_off = n_half // tn
    b3 = b.reshape(b.shape[0], 1, b.shape[1])
    return pl.pallas_call(
        _pw1_glu_kernel,
        grid=(n_half // tn, m // tm),
        in_specs=[
            pl.BlockSpec((tm, d), lambda j, i: (i, 0)),
            pl.BlockSpec((1, 1, d), lambda j, i: (i * tm // seq, 0, 0)),
            pl.BlockSpec((1, 1, d), lambda j, i: (i * tm // seq, 0, 0)),
            pl.BlockSpec((None, d, tn), lambda j, i: (layer, 0, j)),
            pl.BlockSpec((None, d, tn), lambda j, i: (layer, 0, j + gate_off)),
            pl.BlockSpec((None, 1, tn), lambda j, i: (layer, 0, j)),
            pl.BlockSpec((None, 1, tn), lambda j, i: (layer, 0, j + gate_off)),
        ],
        out_specs=pl.BlockSpec((tm, tn), lambda j, i: (i, j)),
        out_shape=jax.ShapeDtypeStruct((m, n_half), F32),
        scratch_shapes=[pltpu.VMEM((d, tn), BF16), pltpu.VMEM((d, tn), BF16)],
        compiler_params=_params("arbitrary", "arbitrary"),
        name="pw1_glu",
    )(x, sc, sh, w, w, b3, b3)


def _conv_ln_silu_kernel(cur_ref, halo_ref, w_ref, bdw_ref, g_ref, b_ref, o_ref, buf, y, *, tiles_per_seq,
                         row_chunk):
    tt, d = cur_ref.shape
    groups = d // LANES
    first = (pl.program_id(0) % tiles_per_seq) == 0
    for c in range(groups):
        lanes = slice(c * LANES, (c + 1) * LANES)
        buf[c, CONV_HALO:, :] = cur_ref[:, lanes]
        buf[c, :CONV_HALO, :] = jnp.where(first, 0.0, halo_ref[:, lanes])

    tap0 = CONV_HALO - (CONV_WIDTH - 1)

    def group_body(c, carry):
        for r in range(tt // row_chunk):
            acc = jnp.zeros((row_chunk, LANES), F32)
            for k in range(CONV_WIDTH):
                acc = acc + buf[c, pl.ds(r * row_chunk + tap0 + k, row_chunk), :] * w_ref[c, pl.ds(k, 1), :]
            y[c, pl.ds(r * row_chunk, row_chunk), :] = acc + bdw_ref[c]
        return carry

    lax.fori_loop(0, groups, group_body, 0)

    total = jnp.zeros((tt, LANES), F32)
    for c in range(groups):
        total = total + y[c]
    mu = jnp.sum(total, axis=-1, keepdims=True) * (1.0 / d)
    sq = jnp.zeros((tt, LANES), F32)
    for c in range(groups):
        yc = y[c] - mu
        sq = sq + yc * yc
    var = jnp.sum(sq, axis=-1, keepdims=True) * (1.0 / d)
    rstd = lax.rsqrt(var + LN_EPS)
    for c in range(groups):
        lanes = slice(c * LANES, (c + 1) * LANES)
        z = (y[c] - mu) * rstd * g_ref[:, lanes] + b_ref[:, lanes]
        o_ref[:, lanes] = (z * jax.nn.sigmoid(z)).astype(o_ref.dtype)


def _conv_ln_silu(u, w_dw, b_dw, ln_g, ln_b, seq, tt=256, row_chunk=64):
    m, d = u.shape
    groups = d // LANES
    halo_per_tile = tt // CONV_HALO
    w_g = jnp.pad(w_dw, ((0, CONV_HALO - CONV_WIDTH), (0, 0))).reshape(CONV_HALO, groups, LANES).transpose(1, 0, 2)
    b_g = b_dw.reshape(groups, 1, LANES)
    kern = functools.partial(_conv_ln_silu_kernel, tiles_per_seq=seq // tt, row_chunk=row_chunk)
    return pl.pallas_call(
        kern,
        grid=(m // tt,),
        in_specs=[
            pl.BlockSpec((tt, d), lambda i: (i, 0)),
            pl.BlockSpec((CONV_HALO, d), lambda i: (jnp.maximum(i * halo_per_tile - 1, 0), 0)),
            pl.BlockSpec((groups, CONV_HALO, LANES), lambda i: (0, 0, 0)),
            pl.BlockSpec((groups, 1, LANES), lambda i: (0, 0, 0)),
            pl.BlockSpec((1, d), lambda i: (0, 0)),
            pl.BlockSpec((1, d), lambda i: (0, 0)),
        ],
        out_specs=pl.BlockSpec((tt, d), lambda i: (i, 0)),
        out_shape=jax.ShapeDtypeStruct((m, d), BF16),
        scratch_shapes=[pltpu.VMEM((groups, CONV_HALO + tt, LANES), F32), pltpu.VMEM((groups, tt, LANES), F32)],
        compiler_params=_params("arbitrary"),
        name="conv_ln_silu",
    )(u, u, w_g, b_g, ln_g.reshape(1, d), ln_b.reshape(1, d))


def _proj_res_ln_kernel(*refs, has_bias):
    if has_bias:
        a_ref, w_ref, bias_ref, x_ref, gate_ref, lng_ref, lnb_ref, o_ref, w_bf = refs
    else:
        a_ref, w_ref, x_ref, gate_ref, lng_ref, lnb_ref, o_ref, w_bf = refs

    @pl.when(pl.program_id(0) == 0)
    def _():
        _cast_rows(w_ref, w_bf)

    y = _dot(a_ref[...], w_bf[...])
    if has_bias:
        y = y + bias_ref[...]
    z = DEEPNORM_ALPHA * x_ref[...] + gate_ref[0] * y
    o_ref[...] = _layer_norm(z, lng_ref[...], lnb_ref[...])


def _proj_res_ln(a, w, bias, x, gate, ln_g, ln_b, w_layer, ln_layer, seq, tm=512):
    m, k = a.shape
    d = w.shape[2]
    in_specs = [
        pl.BlockSpec((tm, k), lambda i: (i, 0)),
        pl.BlockSpec((None, k, d), lambda i: (w_layer, 0, 0), pipeline_mode=pl.Buffered(1)),
    ]
    args = [a, w]
    if bias is not None:
        in_specs.append(pl.BlockSpec((None, 1, d), lambda i: (w_layer, 0, 0)))
        args.append(bias.reshape(bias.shape[0], 1, d))
    in_specs += [
        pl.BlockSpec((tm, d), lambda i: (i, 0)),
        pl.BlockSpec((1, 1, d), lambda i: (i * tm // seq, 0, 0)),
        pl.BlockSpec((None, 1, d), lambda i: (ln_layer, 0, 0)),
        pl.BlockSpec((None, 1, d), lambda i: (ln_layer, 0, 0)),
    ]
    args += [x, gate, ln_g.reshape(-1, 1, d), ln_b.reshape(-1, 1, d)]
    return pl.pallas_call(
        functools.partial(_proj_res_ln_kernel, has_bias=bias is not None),
        grid=(m // tm,),
        in_specs=in_specs,
        out_specs=pl.BlockSpec((tm, d), lambda i: (i, 0)),
        out_shape=jax.ShapeDtypeStruct((m, d), F32),
        scratch_shapes=[pltpu.VMEM((k, d), BF16)],
        compiler_params=_params("arbitrary"),
        name="proj_res_ln",
    )(*args)


def _mlp_kernel(x_ref, sc_ref, sh_ref, gate_ref, w1_ref, b1_ref, w2_ref, b2_ref, lng_ref, lnb_ref, o_ref, h_bf):
    f = pl.program_id(1)

    @pl.when(f == 0)
    def _():
        h_bf[...] = (x_ref[...] * (1.0 + sc_ref[0]) + sh_ref[0]).astype(BF16)
        o_ref[...] = jnp.zeros_like(o_ref)

    u = _dot(h_bf[...], w1_ref[...].astype(BF16)) + b1_ref[...]
    u = jnp.square(jnp.maximum(u, 0.0)).astype(BF16)
    o_ref[...] += _dot(u, w2_ref[...].astype(BF16))

    @pl.when(f == pl.num_programs(1) - 1)
    def _():
        y = o_ref[...] + b2_ref[...]
        z = DEEPNORM_ALPHA * x_ref[...] + gate_ref[0] * y
        o_ref[...] = _layer_norm(z, lng_ref[...], lnb_ref[...])


def _mlp(x, sc, sh, gate, w1, b1, w2, b2, ln_g, ln_b, layer, seq, tm=1024, tf=512):
    m, d = x.shape
    dff = w1.shape[2]
    mod_spec = pl.BlockSpec((1, 1, d), lambda i, f: (i * tm // seq, 0, 0))
    vec_spec = pl.BlockSpec((None, 1, d), lambda i, f: (layer, 0, 0))
    return pl.pallas_call(
        _mlp_kernel,
        grid=(m // tm, dff // tf),
        in_specs=[
            pl.BlockSpec((tm, d), lambda i, f: (i, 0), pipeline_mode=pl.Buffered(1)),
            mod_spec, mod_spec, mod_spec,
            pl.BlockSpec((None, d, tf), lambda i, f: (layer, 0, f)),
            pl.BlockSpec((None, 1, tf), lambda i, f: (layer, 0, f)),
            pl.BlockSpec((None, tf, d), lambda i, f: (layer, f, 0)),
            vec_spec, vec_spec, vec_spec,
        ],
        out_specs=pl.BlockSpec((tm, d), lambda i, f: (i, 0)),
        out_shape=jax.ShapeDtypeStruct((m, d), F32),
        scratch_shapes=[pltpu.VMEM((tm, d), BF16)],
        compiler_params=_params("arbitrary", "arbitrary"),
        name="mlp",
    )(x, sc, sh, gate, w1, b1.reshape(-1, 1, dff), w2, b2.reshape(-1, 1, d),
      ln_g.reshape(-1, 1, d), ln_b.reshape(-1, 1, d))


def _rope_table_kernel(pos_ref, inv_freq_ref, sign_ref, cos_ref, sin_ref):
    ang = pos_ref[...].astype(F32) * inv_freq_ref[...]
    cos_ref[...] = jnp.cos(ang)
    sin_ref[...] = jnp.sin(ang) * sign_ref[...]


def _rope_tables(positions, tm=1024):
    m = positions.size
    half = DIFF_HEAD_DIM // 2
    inv_freq = ROPE_THETA ** (-jnp.arange(0, DIFF_HEAD_DIM, 2, dtype=F32) / DIFF_HEAD_DIM)
    inv_freq = jnp.concatenate([inv_freq, inv_freq]).reshape(1, DIFF_HEAD_DIM)
    sign = jnp.concatenate([-jnp.ones((half,), F32), jnp.ones((half,), F32)]).reshape(1, DIFF_HEAD_DIM)
    pos = jnp.broadcast_to(positions.reshape(m, 1), (m, DIFF_HEAD_DIM))
    tab = jax.ShapeDtypeStruct((m, DIFF_HEAD_DIM), F32)
    return pl.pallas_call(
        _rope_table_kernel,
        grid=(m // tm,),
        in_specs=[
            pl.BlockSpec((tm, DIFF_HEAD_DIM), lambda i: (i, 0)),
            pl.BlockSpec((1, DIFF_HEAD_DIM), lambda i: (0, 0)),
            pl.BlockSpec((1, DIFF_HEAD_DIM), lambda i: (0, 0)),
        ],
        out_specs=[pl.BlockSpec((tm, DIFF_HEAD_DIM), lambda i: (i, 0))] * 2,
        out_shape=[tab, tab],
        compiler_params=_params("arbitrary"),
        name="rope_tables",
    )(pos, inv_freq, sign)


def _qkv_rope_kernel(x_ref, sc_ref, sh_ref, w_ref, cos_ref, sin_ref, o_ref, w_bf, *, q_tiles, rope_tiles):
    j = pl.program_id(0)

    @pl.when(pl.program_id(1) == 0)
    def _():
        _cast_rows(w_ref, w_bf)

    h = (x_ref[...] * (1.0 + sc_ref[0]) + sh_ref[0]).astype(BF16)
    y = _dot(h, w_bf[...])
    tn = y.shape[1]

    @pl.when(j < rope_tiles)
    def _():
        scale = jnp.where(j < q_tiles, DIFF_HEAD_DIM ** -0.5, 1.0)
        cos = cos_ref[...]
        sin = sin_ref[...]
        for c in range(tn // DIFF_HEAD_DIM):
            lanes = slice(c * DIFF_HEAD_DIM, (c + 1) * DIFF_HEAD_DIM)
            t = y[:, lanes]
            r = t * cos + pltpu.roll(t, DIFF_HEAD_DIM // 2, 1) * sin
            o_ref[:, lanes] = (r * scale).astype(o_ref.dtype)

    @pl.when(j >= rope_tiles)
    def _():
        o_ref[...] = y.astype(o_ref.dtype)


def _qkv_rope(x, sc, sh, w, cos, sin, layer, seq, tm=512, tn=1024):
    m, d = x.shape
    n = w.shape[2]
    kern = functools.partial(_qkv_rope_kernel, q_tiles=D_MODEL // tn, rope_tiles=2 * D_MODEL // tn)
    return pl.pallas_call(
        kern,
        grid=(n // tn, m // tm),
        in_specs=[
            pl.BlockSpec((tm, d), lambda j, i: (i, 0)),
            pl.BlockSpec((1, 1, d), lambda j, i: (i * tm // seq, 0, 0)),
            pl.BlockSpec((1, 1, d), lambda j, i: (i * tm // seq, 0, 0)),
            pl.BlockSpec((None, d, tn), lambda j, i: (layer, 0, j)),
            pl.BlockSpec((tm, DIFF_HEAD_DIM), lambda j, i: (i, 0)),
            pl.BlockSpec((tm, DIFF_HEAD_DIM), lambda j, i: (i, 0)),
        ],
        out_specs=pl.BlockSpec((tm, tn), lambda j, i: (i, j)),
        out_shape=jax.ShapeDtypeStruct((m, n), BF16),
        scratch_shapes=[pltpu.VMEM((d, tn), BF16)],
        compiler_params=_params("arbitrary", "arbitrary"),
        name="qkv_rope",
    )(x, sc, sh, w, cos, sin)


def _diff_attn_kernel(q_ref, k_ref, v_ref, lq1_ref, lk1_ref, lq2_ref, lk2_ref, g_ref, o_ref, *, lambda_init):
    tq = q_ref.shape[0]
    dh = DIFF_HEAD_DIM
    lam = (jnp.exp(jnp.sum(lq1_ref[...] * lk1_ref[...], axis=-1, keepdims=True))
           - jnp.exp(jnp.sum(lq2_ref[...] * lk2_ref[...], axis=-1, keepdims=True)) + lambda_init)

    def tile(n_chunks):
        kend = n_chunks * tq

        def run():
            row = lax.broadcasted_iota(jnp.int32, (tq, tq), 0)
            col = lax.broadcasted_iota(jnp.int32, (tq, tq), 1)
            a = None
            for mp in range(2):
                q = q_ref[:, mp * dh:(mp + 1) * dh]
                k = k_ref[0:kend, mp * dh:(mp + 1) * dh]
                s = lax.dot_general(q, k, (((1,), (1,)), ((), ())), preferred_element_type=F32)
                diag = jnp.where(row >= col, s[:, kend - tq:], -jnp.inf)
                s = diag if n_chunks == 1 else jnp.concatenate([s[:, :kend - tq], diag], axis=1)
                p = jnp.exp(s - jnp.max(s, axis=-1, keepdims=True))
                inv_l = 1.0 / jnp.sum(p, axis=-1, keepdims=True)
                a = p * inv_l if mp == 0 else a - p * (lam * inv_l)
            o = _dot(a.astype(v_ref.dtype), v_ref[0:kend, :])
            o = o * lax.rsqrt(jnp.mean(o * o, axis=-1, keepdims=True) + RMS_EPS)
            o = o * g_ref[...] * (1.0 - lambda_init)
            o_ref[...] = o.astype(o_ref.dtype)

        return run

    i = pl.program_id(2)
    for n_chunks in range(1, k_ref.shape[0] // tq + 1):
        pl.when(i == n_chunks - 1)(tile(n_chunks))


def _diff_attn(qkv, lq1, lk1, lq2, lk2, subln_g, layer, bsz, seq, lambda_init, tq=256):
    m = qkv.shape[0]
    nq = seq // tq
    vd = DIFF_V_DIM
    lam_spec = pl.BlockSpec((None, 1, DIFF_HEAD_DIM), lambda b, h, i: (layer, 0, 0))
    lam_args = [a.reshape(-1, 1, DIFF_HEAD_DIM) for a in (lq1, lk1, lq2, lk2)]
    return pl.pallas_call(
        functools.partial(_diff_attn_kernel, lambda_init=lambda_init),
        grid=(bsz, DIFF_HEADS, nq),
        in_specs=[
            pl.BlockSpec((tq, vd), lambda b, h, i: (b * nq + i, h)),
            pl.BlockSpec((seq, vd), lambda b, h, i: (b, DIFF_HEADS + h)),
            pl.BlockSpec((seq, vd), lambda b, h, i: (b, 2 * DIFF_HEADS + h)),
            lam_spec, lam_spec, lam_spec, lam_spec,
            pl.BlockSpec((None, 1, vd), lambda b, h, i: (layer, 0, 0)),
        ],
        out_specs=pl.BlockSpec((tq, vd), lambda b, h, i: (b * nq + i, h)),
        out_shape=jax.ShapeDtypeStruct((m, DIFF_HEADS * vd), BF16),
        compiler_params=_params("arbitrary", "arbitrary", "arbitrary"),
        name="diff_attn",
    )(qkv, qkv, qkv, *lam_args, subln_g.reshape(-1, 1, vd))


def kernel(x, c, positions, ada_w, ada_b, ln_mix_g, ln_mix_b, ln_ffn_g, ln_ffn_b, conv_pw1_w, conv_pw1_b, conv_dw_w, conv_dw_b, conv_ln_g, conv_ln_b, conv_pw2_w, conv_pw2_b, attn_qkv_w, attn_lq1, attn_lk1, attn_lq2, attn_lk2, attn_subln_g, attn_o_w, mlp_w1, mlp_b1, mlp_w2, mlp_b2):
    bsz, seq, d = x.shape
    xf = x.reshape(bsz * seq, d)
    mod = _adaln(c, ada_w, ada_b).reshape(DEPTH, bsz, 6, 1, d)
    cos, sin = _rope_tables(positions)
    for i in range(DEPTH):
        sh_m, sc_m, g_m, sh_f, sc_f, g_f = [mod[i, :, k] for k in range(6)]
        j = i // N_MIXERS
        if i % N_MIXERS == 0:
            u = _pw1_glu(xf, sc_m, sh_m, conv_pw1_w, conv_pw1_b, j, seq)
            v = _conv_ln_silu(u, conv_dw_w[j], conv_dw_b[j], conv_ln_g[j], conv_ln_b[j], seq)
            xf = _proj_res_ln(v, conv_pw2_w, conv_pw2_b, xf, g_m, ln_mix_g, ln_mix_b, j, i, seq)
        else:
            lambda_init = 0.8 - 0.6 * math.exp(-0.3 * i)
            qkv = _qkv_rope(xf, sc_m, sh_m, attn_qkv_w, cos, sin, j, seq)
            o = _diff_attn(qkv, attn_lq1, attn_lk1, attn_lq2, attn_lk2, attn_subln_g, j, bsz, seq, lambda_init)
            xf = _proj_res_ln(o, attn_o_w, None, xf, g_m, ln_mix_g, ln_mix_b, j, i, seq)
        xf = _mlp(xf, sc_f, sh_f, g_f, mlp_w1, mlp_b1, mlp_w2, mlp_b2, ln_ffn_g, ln_ffn_b, i, seq)
    return xf.reshape(bsz, seq, d)
```

```python
import functools
import math

import jax
import jax.numpy as jnp
from jax import lax
from jax.experimental import pallas as pl
from jax.experimental.pallas import tpu as pltpu

F32 = jnp.float32
BF16 = jnp.bfloat16

D_MODEL = 2048
DEPTH = 2
N_MIXERS = 2
CONV_WIDTH = 31
DIFF_HEADS = 8
DIFF_HEAD_DIM = D_MODEL // (2 * DIFF_HEADS)
DIFF_V_DIM = 2 * DIFF_HEAD_DIM
D_FF = 4 * D_MODEL
ROPE_THETA = 10000.0
DEEPNORM_ALPHA = (2.0 * DEPTH) ** 0.25
LN_EPS = 1e-5
RMS_EPS = 1e-5
LOG2_E = math.log2(math.e)

LANES = 128
SUB_ROWS = 256
CONV_HALO = 32
VMEM_LIMIT = 60 * 1024 * 1024


def _dot(a, b):
    return jnp.dot(a, b, preferred_element_type=F32)


def _layer_norm(z, g, b):
    mu = jnp.mean(z, axis=-1, keepdims=True)
    zc = z - mu
    var = jnp.mean(zc * zc, axis=-1, keepdims=True)
    return zc * lax.rsqrt(var + LN_EPS) * g + b


def _cast_rows(src_ref, dst_ref, rows_per_step=256):
    n = src_ref.shape[0] // rows_per_step

    def body(i, carry):
        r = pl.multiple_of(i * rows_per_step, rows_per_step)
        dst_ref[pl.ds(r, rows_per_step), :] = src_ref[pl.ds(r, rows_per_step), :].astype(dst_ref.dtype)
        return carry

    lax.fori_loop(0, n, body, 0)


def _params(*sem):
    return pltpu.CompilerParams(dimension_semantics=sem, vmem_limit_bytes=VMEM_LIMIT)


def _adaln_kernel(c_ref, w_ref, b_ref, o_ref):
    c = c_ref[...]
    cond = (c * jax.nn.sigmoid(c)).astype(BF16)
    o_ref[...] = _dot(cond, w_ref[...].astype(BF16)) + b_ref[...]


def _adaln(c, ada_w, ada_b, tn=1024):
    depth, d, n = ada_w.shape
    bsz = c.shape[0]
    return pl.pallas_call(
        _adaln_kernel,
        grid=(depth, n // tn),
        in_specs=[
            pl.BlockSpec((bsz, d), lambda i, j: (0, 0)),
            pl.BlockSpec((None, d, tn), lambda i, j: (i, 0, j)),
            pl.BlockSpec((None, 1, tn), lambda i, j: (i, 0, j)),
        ],
        out_specs=pl.BlockSpec((None, bsz, tn), lambda i, j: (i, 0, j)),
        out_shape=jax.ShapeDtypeStruct((depth, bsz, n), F32),
        compiler_params=_params("arbitrary", "arbitrary"),
        name="adaln",
    )(c, ada_w, ada_b.reshape(depth, 1, n))


def _pw1_glu_kernel(x_ref, sc_ref, sh_ref, wa_ref, wg_ref, ba_ref, bg_ref, o_ref, wa_bf, wg_bf):
    @pl.when(pl.program_id(1) == 0)
    def _():
        _cast_rows(wa_ref, wa_bf)
        _cast_rows(wg_ref, wg_bf)

    for r in range(0, x_ref.shape[0], SUB_ROWS):
        rows = slice(r, r + SUB_ROWS)
        h = (x_ref[rows, :] * (1.0 + sc_ref[0]) + sh_ref[0]).astype(BF16)
        a = _dot(h, wa_bf[...]) + ba_ref[...]
        g = _dot(h, wg_bf[...]) + bg_ref[...]
        o_ref[rows, :] = a * jax.nn.sigmoid(g)


def _pw1_glu(x, sc, sh, w, b, layer, seq, tm=1024, tn=512):
    m, d = x.shape
    n_half = w.shape[2] // 2
    gate_off = n_half // tn
    b3 = b.reshape(b.shape[0], 1, b.shape[1])
    return pl.pallas_call(
        _pw1_glu_kernel,
        grid=(n_half // tn, m // tm),
        in_specs=[
            pl.BlockSpec((tm, d), lambda j, i: (i, 0)),
            pl.BlockSpec((1, 1, d), lambda j, i: (i * tm // seq, 0, 0)),
            pl.BlockSpec((1, 1, d), lambda j, i: (i * tm // seq, 0, 0)),
            pl.BlockSpec((None, d, tn), lambda j, i: (layer, 0, j)),
            pl.BlockSpec((None, d, tn), lambda j, i: (layer, 0, j + gate_off)),
            pl.BlockSpec((None, 1, tn), lambda j, i: (layer, 0, j)),
            pl.BlockSpec((None, 1, tn), lambda j, i: (layer, 0, j + gate_off)),
        ],
        out_specs=pl.BlockSpec((tm, tn), lambda j, i: (i, j)),
        out_shape=jax.ShapeDtypeStruct((m, n_half), F32),
        scratch_shapes=[pltpu.VMEM((d, tn), BF16), pltpu.VMEM((d, tn), BF16)],
        compiler_params=_params("arbitrary", "arbitrary"),
        name="pw1_glu",
    )(x, sc, sh, w, w, b3, b3)


def _conv_ln_silu_kernel(cur_ref, halo_ref, w_ref, bdw_ref, g_ref, b_ref, o_ref, buf, y, *, tiles_per_seq,
                         row_chunk):
    tt, d = cur_ref.shape
    groups = d // LANES
    first = (pl.program_id(0) % tiles_per_seq) == 0
    for c in range(groups):
        lanes = slice(c * LANES, (c + 1) * LANES)
        buf[c, CONV_HALO:, :] = cur_ref[:, lanes]
        buf[c, :CONV_HALO, :] = jnp.where(first, 0.0, halo_ref[:, lanes])

    tap0 = CONV_HALO - (CONV_WIDTH - 1)

    def group_body(c, carry):
        for r in range(tt // row_chunk):
            acc = jnp.zeros((row_chunk, LANES), F32)
            for k in range(CONV_WIDTH):
                acc = acc + buf[c, pl.ds(r * row_chunk + tap0 + k, row_chunk), :] * w_ref[c, pl.ds(k, 1), :]
            y[c, pl.ds(r * row_chunk, row_chunk), :] = acc + bdw_ref[c]
        return carry

    lax.fori_loop(0, groups, group_body, 0)

    total = jnp.zeros((tt, LANES), F32)
    for c in range(groups):
        total = total + y[c]
    mu = jnp.sum(total, axis=-1, keepdims=True) * (1.0 / d)
    sq = jnp.zeros((tt, LANES), F32)
    for c in range(groups):
        yc = y[c] - mu
        sq = sq + yc * yc
    var = jnp.sum(sq, axis=-1, keepdims=True) * (1.0 / d)
    rstd = lax.rsqrt(var + LN_EPS)
    for c in range(groups):
        lanes = slice(c * LANES, (c + 1) * LANES)
        z = (y[c] - mu) * rstd * g_ref[:, lanes] + b_ref[:, lanes]
        o_ref[:, lanes] = (z * jax.nn.sigmoid(z)).astype(o_ref.dtype)


def _conv_ln_silu(u, w_dw, b_dw, ln_g, ln_b, seq, tt=256, row_chunk=64):
    m, d = u.shape
    groups = d // LANES
    halo_per_tile = tt // CONV_HALO
    w_g = jnp.pad(w_dw, ((0, CONV_HALO - CONV_WIDTH), (0, 0))).reshape(CONV_HALO, groups, LANES).transpose(1, 0, 2)
    b_g = b_dw.reshape(groups, 1, LANES)
    kern = functools.partial(_conv_ln_silu_kernel, tiles_per_seq=seq // tt, row_chunk=row_chunk)
    return pl.pallas_call(
        kern,
        grid=(m // tt,),
        in_specs=[
            pl.BlockSpec((tt, d), lambda i: (i, 0)),
            pl.BlockSpec((CONV_HALO, d), lambda i: (jnp.maximum(i * halo_per_tile - 1, 0), 0)),
            pl.BlockSpec((groups, CONV_HALO, LANES), lambda i: (0, 0, 0)),
            pl.BlockSpec((groups, 1, LANES), lambda i: (0, 0, 0)),
            pl.BlockSpec((1, d), lambda i: (0, 0)),
            pl.BlockSpec((1, d), lambda i: (0, 0)),
        ],
        out_specs=pl.BlockSpec((tt, d), lambda i: (i, 0)),
        out_shape=jax.ShapeDtypeStruct((m, d), BF16),
        scratch_shapes=[pltpu.VMEM((groups, CONV_HALO + tt, LANES), F32), pltpu.VMEM((groups, tt, LANES), F32)],
        compiler_params=_params("arbitrary"),
        name="conv_ln_silu",
    )(u, u, w_g, b_g, ln_g.reshape(1, d), ln_b.reshape(1, d))


def _proj_res_ln_kernel(*refs, has_bias, n_cast):
    if has_bias:
        a_ref, w_ref, bias_ref, x_ref, gate_ref, lng_ref, lnb_ref, o_ref, w_bf = refs
    else:
        a_ref, w_ref, x_ref, gate_ref, lng_ref, lnb_ref, o_ref, w_bf = refs
    s = pl.program_id(0)
    chunk_rows = w_ref.shape[0]

    @pl.when(s < n_cast)
    def _():
        r = pl.multiple_of(s * chunk_rows, chunk_rows)
        w_bf[pl.ds(r, chunk_rows), :] = w_ref[...].astype(BF16)

    @pl.when(s >= n_cast)
    def _():
        w = w_bf[...]
        for r in range(0, a_ref.shape[0], SUB_ROWS):
            rows = slice(r, r + SUB_ROWS)
            y = _dot(a_ref[rows, :], w)
            if has_bias:
                y = y + bias_ref[...]
            z = DEEPNORM_ALPHA * x_ref[rows, :] + gate_ref[0] * y
            o_ref[rows, :] = _layer_norm(z, lng_ref[...], lnb_ref[...])


def _proj_res_ln(a, w, bias, x, gate, ln_g, ln_b, w_layer, ln_layer, seq, tm=512, chunk_rows=256):
    m, k = a.shape
    d = w.shape[2]
    n_cast = k // chunk_rows

    def tile(s):
        return jnp.maximum(s - n_cast, 0)

    in_specs = [
        pl.BlockSpec((tm, k), lambda s: (tile(s), 0)),
        pl.BlockSpec((None, chunk_rows, d), lambda s: (w_layer, jnp.minimum(s, n_cast - 1), 0)),
    ]
    args = [a, w]
    if bias is not None:
        in_specs.append(pl.BlockSpec((None, 1, d), lambda s: (w_layer, 0, 0)))
        args.append(bias.reshape(bias.shape[0], 1, d))
    in_specs += [
        pl.BlockSpec((tm, d), lambda s: (tile(s), 0)),
        pl.BlockSpec((1, 1, d), lambda s: (tile(s) * tm // seq, 0, 0)),
        pl.BlockSpec((None, 1, d), lambda s: (ln_layer, 0, 0)),
        pl.BlockSpec((None, 1, d), lambda s: (ln_layer, 0, 0)),
    ]
    args += [x, gate, ln_g.reshape(-1, 1, d), ln_b.reshape(-1, 1, d)]
    return pl.pallas_call(
        functools.partial(_proj_res_ln_kernel, has_bias=bias is not None, n_cast=n_cast),
        grid=(n_cast + m // tm,),
        in_specs=in_specs,
        out_specs=pl.BlockSpec((tm, d), lambda s: (tile(s), 0)),
        out_shape=jax.ShapeDtypeStruct((m, d), F32),
        scratch_shapes=[pltpu.VMEM((k, d), BF16)],
        compiler_params=_params("arbitrary"),
        name="proj_res_ln",
    )(*args)


def _mlp_kernel(x_ref, sc_ref, sh_ref, gate_ref, w1_ref, b1_ref, w2_ref, b2_ref, lng_ref, lnb_ref, o_ref, h_bf):
    f = pl.program_id(1)
    last = pl.num_programs(1) - 1

    def chunk(h, w1, w2):
        u = _dot(h, w1) + b1_ref[...]
        return _dot(jnp.square(jnp.maximum(u, 0.0)).astype(BF16), w2)

    @pl.when(f == 0)
    def _():
        w1, w2 = w1_ref[...].astype(BF16), w2_ref[...].astype(BF16)
        for r in range(0, x_ref.shape[0], SUB_ROWS):
            rows = slice(r, r + SUB_ROWS)
            h = (x_ref[rows, :] * (1.0 + sc_ref[0]) + sh_ref[0]).astype(BF16)
            h_bf[rows, :] = h
            o_ref[rows, :] = chunk(h, w1, w2)

    @pl.when(jnp.logical_and(f > 0, f < last))
    def _():
        o_ref[...] += chunk(h_bf[...], w1_ref[...].astype(BF16), w2_ref[...].astype(BF16))

    @pl.when(f == last)
    def _():
        w1, w2 = w1_ref[...].astype(BF16), w2_ref[...].astype(BF16)
        for r in range(0, x_ref.shape[0], SUB_ROWS):
            rows = slice(r, r + SUB_ROWS)
            y = o_ref[rows, :] + chunk(h_bf[rows, :], w1, w2) + b2_ref[...]
            z = DEEPNORM_ALPHA * x_ref[rows, :] + gate_ref[0] * y
            o_ref[rows, :] = _layer_norm(z, lng_ref[...], lnb_ref[...])


def _mlp(x, sc, sh, gate, w1, b1, w2, b2, ln_g, ln_b, layer, seq, tm=1024, tf=512):
    m, d = x.shape
    dff = w1.shape[2]
    mod_spec = pl.BlockSpec((1, 1, d), lambda i, f: (i * tm // seq, 0, 0))
    vec_spec = pl.BlockSpec((None, 1, d), lambda i, f: (layer, 0, 0))
    return pl.pallas_call(
        _mlp_kernel,
        grid=(m // tm, dff // tf),
        in_specs=[
            pl.BlockSpec((tm, d), lambda i, f: (i, 0), pipeline_mode=pl.Buffered(1)),
            mod_spec, mod_spec, mod_spec,
            pl.BlockSpec((None, d, tf), lambda i, f: (layer, 0, f)),
            pl.BlockSpec((None, 1, tf), lambda i, f: (layer, 0, f)),
            pl.BlockSpec((None, tf, d), lambda i, f: (layer, f, 0)),
            vec_spec, vec_spec, vec_spec,
        ],
        out_specs=pl.BlockSpec((tm, d), lambda i, f: (i, 0)),
        out_shape=jax.ShapeDtypeStruct((m, d), F32),
        scratch_shapes=[pltpu.VMEM((tm, d), BF16)],
        compiler_params=_params("arbitrary", "arbitrary"),
        name="mlp",
    )(x, sc, sh, gate, w1, b1.reshape(-1, 1, dff), w2, b2.reshape(-1, 1, d),
      ln_g.reshape(-1, 1, d), ln_b.reshape(-1, 1, d))


def _rope_table_kernel(pos_ref, inv_freq_ref, sign_ref, cos_ref, sin_ref):
    ang = pos_ref[...].astype(F32) * inv_freq_ref[...]
    cos_ref[...] = jnp.cos(ang)
    sin_ref[...] = jnp.sin(ang) * sign_ref[...]


def _rope_tables(positions, tm=1024):
    m = positions.size
    half = DIFF_HEAD_DIM // 2
    inv_freq = ROPE_THETA ** (-jnp.arange(0, DIFF_HEAD_DIM, 2, dtype=F32) / DIFF_HEAD_DIM)
    inv_freq = jnp.concatenate([inv_freq, inv_freq]).reshape(1, DIFF_HEAD_DIM)
    sign = jnp.concatenate([-jnp.ones((half,), F32), jnp.ones((half,), F32)]).reshape(1, DIFF_HEAD_DIM)
    pos = jnp.broadcast_to(positions.reshape(m, 1), (m, DIFF_HEAD_DIM))
    tab = jax.ShapeDtypeStruct((m, DIFF_HEAD_DIM), F32)
    return pl.pallas_call(
        _rope_table_kernel,
        grid=(m // tm,),
        in_specs=[
            pl.BlockSpec((tm, DIFF_HEAD_DIM), lambda i: (i, 0)),
            pl.BlockSpec((1, DIFF_HEAD_DIM), lambda i: (0, 0)),
            pl.BlockSpec((1, DIFF_HEAD_DIM), lambda i: (0, 0)),
        ],
        out_specs=[pl.BlockSpec((tm, DIFF_HEAD_DIM), lambda i: (i, 0))] * 2,
        out_shape=[tab, tab],
        compiler_params=_params("arbitrary"),
        name="rope_tables",
    )(pos, inv_freq, sign)


def _qkv_rope_kernel(x_ref, sc_ref, sh_ref, w_ref, cos_ref, sin_ref, o_ref, w_bf, *, q_tiles, rope_tiles):
    j = pl.program_id(0)

    @pl.when(pl.program_id(1) == 0)
    def _():
        _cast_rows(w_ref, w_bf)

    tn = o_ref.shape[1]

    def step(rope):
        def run():
            scale = jnp.where(j < q_tiles, DIFF_HEAD_DIM ** -0.5 * LOG2_E, 1.0)
            for r in range(0, x_ref.shape[0], SUB_ROWS):
                rows = slice(r, r + SUB_ROWS)
                h = (x_ref[rows, :] * (1.0 + sc_ref[0]) + sh_ref[0]).astype(BF16)
                y = _dot(h, w_bf[...])
                if not rope:
                    o_ref[rows, :] = y.astype(o_ref.dtype)
                    continue
                cos = cos_ref[rows, :] * scale
                sin = sin_ref[rows, :] * scale
                for c in range(tn // DIFF_HEAD_DIM):
                    lanes = slice(c * DIFF_HEAD_DIM, (c + 1) * DIFF_HEAD_DIM)
                    t = y[:, lanes]
                    o_ref[rows, lanes] = (t * cos + pltpu.roll(t, DIFF_HEAD_DIM // 2, 1) * sin).astype(o_ref.dtype)

        return run

    pl.when(j < rope_tiles)(step(True))
    pl.when(j >= rope_tiles)(step(False))


def _qkv_rope(x, sc, sh, w, cos, sin, layer, seq, tm=1024, tn=1024):
    m, d = x.shape
    n = w.shape[2]
    kern = functools.partial(_qkv_rope_kernel, q_tiles=D_MODEL // tn, rope_tiles=2 * D_MODEL // tn)
    return pl.pallas_call(
        kern,
        grid=(n // tn, m // tm),
        in_specs=[
            pl.BlockSpec((tm, d), lambda j, i: (i, 0)),
            pl.BlockSpec((1, 1, d), lambda j, i: (i * tm // seq, 0, 0)),
            pl.BlockSpec((1, 1, d), lambda j, i: (i * tm // seq, 0, 0)),
            pl.BlockSpec((None, d, tn), lambda j, i: (layer, 0, j)),
            pl.BlockSpec((tm, DIFF_HEAD_DIM), lambda j, i: (i, 0)),
            pl.BlockSpec((tm, DIFF_HEAD_DIM), lambda j, i: (i, 0)),
        ],
        out_specs=pl.BlockSpec((tm, tn), lambda j, i: (i, j)),
        out_shape=jax.ShapeDtypeStruct((m, n), BF16),
        scratch_shapes=[pltpu.VMEM((d, tn), BF16)],
        compiler_params=_params("arbitrary", "arbitrary"),
        name="qkv_rope",
    )(x, sc, sh, w, cos, sin)


def _diff_attn_kernel(q_ref, k_ref, v_ref, lq1_ref, lk1_ref, lq2_ref, lk2_ref, g_ref, o_ref, *, lambda_init):
    tq = q_ref.shape[0]
    dh = DIFF_HEAD_DIM
    vd = DIFF_V_DIM
    lam = (jnp.exp(jnp.sum(lq1_ref[...] * lk1_ref[...], axis=-1, keepdims=True))
           - jnp.exp(jnp.sum(lq2_ref[...] * lk2_ref[...], axis=-1, keepdims=True)) + lambda_init)

    def head_stages(e, n_chunks, row, col):
        st = {"s": [[None] * n_chunks, [None] * n_chunks], "p": [[None] * n_chunks, [None] * n_chunks],
              "mx": [None, None], "sum": [None, None], "coef": [None, None], "acc": None}

        def scores(c):
            def step():
                for mp in range(2):
                    lanes = slice(e * vd + mp * dh, e * vd + (mp + 1) * dh)
                    s = lax.dot_general(q_ref[:, lanes], k_ref[c * tq:(c + 1) * tq, lanes], (((1,), (1,)), ((), ())),
                                        preferred_element_type=F32)
                    if c == n_chunks - 1:
                        s = jnp.where(row >= col, s, -jnp.inf)
                    st["s"][mp][c] = s
                    part = jnp.maximum(s[:, :LANES], s[:, LANES:])
                    st["mx"][mp] = part if c == 0 else jnp.maximum(st["mx"][mp], part)
            return step

        def row_max():
            for mp in range(2):
                st["mx"][mp] = jnp.max(st["mx"][mp], axis=-1, keepdims=True)

        def exps(c):
            def step():
                for mp in range(2):
                    p = jnp.exp2(st["s"][mp][c] - st["mx"][mp])
                    st["p"][mp][c] = p
                    part = p[:, :LANES] + p[:, LANES:]
                    st["sum"][mp] = part if c == 0 else st["sum"][mp] + part
            return step

        def coefs():
            l1, l2 = [jnp.sum(st["sum"][mp], axis=-1, keepdims=True) for mp in range(2)]
            st["coef"] = [1.0 / l1, lam * l1 / l2]

        def values(c):
            def step():
                a = st["p"][0][c] - st["p"][1][c] * st["coef"][1]
                o = _dot(a.astype(v_ref.dtype), v_ref[c * tq:(c + 1) * tq, e * vd:(e + 1) * vd])
                st["acc"] = o if c == 0 else st["acc"] + o
            return step

        def finish():
            o = st["acc"] * st["coef"][0]
            o = o * lax.rsqrt(jnp.mean(o * o, axis=-1, keepdims=True) + RMS_EPS)
            o = o * g_ref[...] * (1.0 - lambda_init)
            o_ref[:, e * vd:(e + 1) * vd] = o.astype(o_ref.dtype)

        chunks = range(n_chunks)
        return ([scores(c) for c in chunks],
                [row_max] + [exps(c) for c in chunks] + [coefs],
                [values(c) for c in chunks] + [finish])

    def tile(n_chunks):
        def run():
            row = lax.broadcasted_iota(jnp.int32, (tq, tq), 0)
            col = lax.broadcasted_iota(jnp.int32, (tq, tq), 1)
            heads = [head_stages(e, n_chunks, row, col) for e in range(q_ref.shape[1] // vd)]
            for t in range(len(heads) + 2):
                live = [heads[t - k][k] for k in range(3) if 0 <= t - k < len(heads)]
                for j in range(max(len(steps) for steps in live)):
                    for steps in live:
                        if j < len(steps):
                            steps[j]()

        return run

    i = pl.program_id(2)
    for n_chunks in range(1, k_ref.shape[0] // tq + 1):
        pl.when(i == n_chunks - 1)(tile(n_chunks))


def _diff_attn(qkv, lq1, lk1, lq2, lk2, subln_g, layer, bsz, seq, lambda_init, tq=256, heads_per_step=4):
    m = qkv.shape[0]
    nq = seq // tq
    vd = DIFF_V_DIM
    gw = heads_per_step * vd
    groups = DIFF_HEADS // heads_per_step
    lam_spec = pl.BlockSpec((None, 1, DIFF_HEAD_DIM), lambda b, h, i: (layer, 0, 0))
    lam_args = [a.reshape(-1, 1, DIFF_HEAD_DIM) for a in (lq1, lk1, lq2, lk2)]
    return pl.pallas_call(
        functools.partial(_diff_attn_kernel, lambda_init=lambda_init),
        grid=(bsz, groups, nq),
        in_specs=[
            pl.BlockSpec((tq, gw), lambda b, h, i: (b * nq + i, h)),
            pl.BlockSpec((seq, gw), lambda b, h, i: (b, groups + h)),
            pl.BlockSpec((seq, gw), lambda b, h, i: (b, 2 * groups + h)),
            lam_spec, lam_spec, lam_spec, lam_spec,
            pl.BlockSpec((None, 1, vd), lambda b, h, i: (layer, 0, 0)),
        ],
        out_specs=pl.BlockSpec((tq, gw), lambda b, h, i: (b * nq + i, h)),
        out_shape=jax.ShapeDtypeStruct((m, DIFF_HEADS * vd), BF16),
        compiler_params=_params("arbitrary", "arbitrary", "arbitrary"),
        name="diff_attn",
    )(qkv, qkv, qkv, *lam_args, subln_g.reshape(-1, 1, vd))


def kernel(x, c, positions, ada_w, ada_b, ln_mix_g, ln_mix_b, ln_ffn_g, ln_ffn_b, conv_pw1_w, conv_pw1_b, conv_dw_w, conv_dw_b, conv_ln_g, conv_ln_b, conv_pw2_w, conv_pw2_b, attn_qkv_w, attn_lq1, attn_lk1, attn_lq2, attn_lk2, attn_subln_g, attn_o_w, mlp_w1, mlp_b1, mlp_w2, mlp_b2):
    bsz, seq, d = x.shape
    xf = x.reshape(bsz * seq, d)
    mod = _adaln(c, ada_w, ada_b).reshape(DEPTH, bsz, 6, 1, d)
    cos, sin = _rope_tables(positions)
    for i in range(DEPTH):
        sh_m, sc_m, g_m, sh_f, sc_f, g_f = [mod[i, :, k] for k in range(6)]
        j = i // N_MIXERS
        if i % N_MIXERS == 0:
            u = _pw1_glu(xf, sc_m, sh_m, conv_pw1_w, conv_pw1_b, j, seq)
            v = _conv_ln_silu(u, conv_dw_w[j], conv_dw_b[j], conv_ln_g[j], conv_ln_b[j], seq)
            xf = _proj_res_ln(v, conv_pw2_w, conv_pw2_b, xf, g_m, ln_mix_g, ln_mix_b, j, i, seq)
        else:
            lambda_init = 0.8 - 0.6 * math.exp(-0.3 * i)
            qkv = _qkv_rope(xf, sc_m, sh_m, attn_qkv_w, cos, sin, j, seq)
            o = _diff_attn(qkv, attn_lq1, attn_lk1, attn_lq2, attn_lk2, attn_subln_g, j, bsz, seq, lambda_init)
            xf = _proj_res_ln(o, attn_o_w, None, xf, g_m, ln_mix_g, ln_mix_b, j, i, seq)
        xf = _mlp(xf, sc_f, sh_f, g_f, mlp_w1, mlp_b1, mlp_w2, mlp_b2, ln_ffn_g, ln_ffn_b, i, seq)
    return xf.reshape(bsz, seq, d)
```

```python
import functools
import math

import jax
import jax.numpy as jnp
from jax import lax
from jax.experimental import pallas as pl
from jax.experimental.pallas import tpu as pltpu

F32 = jnp.float32
BF16 = jnp.bfloat16

D_MODEL = 2048
DEPTH = 2
N_MIXERS = 2
CONV_WIDTH = 31
DIFF_HEADS = 8
DIFF_HEAD_DIM = D_MODEL // (2 * DIFF_HEADS)
DIFF_V_DIM = 2 * DIFF_HEAD_DIM
D_FF = 4 * D_MODEL
ROPE_THETA = 10000.0
DEEPNORM_ALPHA = (2.0 * DEPTH) ** 0.25
LN_EPS = 1e-5
RMS_EPS = 1e-5
LOG2_E = math.log2(math.e)

LANES = 128
SUB_ROWS = 256
CONV_HALO = 32
VMEM_LIMIT = 60 * 1024 * 1024


def _dot(a, b):
    return jnp.dot(a, b, preferred_element_type=F32)


def _layer_norm(z, g, b):
    mu = jnp.mean(z, axis=-1, keepdims=True)
    zc = z - mu
    var = jnp.mean(zc * zc, axis=-1, keepdims=True)
    return zc * lax.rsqrt(var + LN_EPS) * g + b


def _cast_rows(src_ref, dst_ref, rows_per_step=256):
    n = src_ref.shape[0] // rows_per_step

    def body(i, carry):
        r = pl.multiple_of(i * rows_per_step, rows_per_step)
        dst_ref[pl.ds(r, rows_per_step), :] = src_ref[pl.ds(r, rows_per_step), :].astype(dst_ref.dtype)
        return carry

    lax.fori_loop(0, n, body, 0)


def _params(*sem):
    return pltpu.CompilerParams(dimension_semantics=sem, vmem_limit_bytes=VMEM_LIMIT)


def _mod_spec(mod, batch_of):
    arr, layer, comp = mod
    return pl.BlockSpec((None, None, 1, 1, arr.shape[-1]), lambda *g: (layer, batch_of(*g), comp, 0, 0))


def _adaln_rope_kernel(c_ref, w_ref, b_ref, pos_ref, inv_freq_ref, sign_ref, o_ref, cos_ref, sin_ref, *,
                       rope_steps):
    c = c_ref[...]
    cond = (c * jax.nn.sigmoid(c)).astype(BF16)
    o_ref[...] = _dot(cond, w_ref[...].astype(BF16)) + b_ref[...]

    @pl.when(pl.program_id(0) < rope_steps)
    def _():
        ang = pos_ref[...].astype(F32) * inv_freq_ref[...]
        cos_ref[...] = jnp.cos(ang)
        sin_ref[...] = jnp.sin(ang) * sign_ref[...]


def _adaln_rope(c, ada_w, ada_b, positions, tn=1024, tm=1024):
    depth, d, n = ada_w.shape
    bsz = c.shape[0]
    col_tiles = n // tn
    m = positions.size
    rope_steps = m // tm
    assert rope_steps <= depth * col_tiles
    half = DIFF_HEAD_DIM // 2
    inv_freq = ROPE_THETA ** (-jnp.arange(0, DIFF_HEAD_DIM, 2, dtype=F32) / DIFF_HEAD_DIM)
    inv_freq = jnp.concatenate([inv_freq, inv_freq]).reshape(1, DIFF_HEAD_DIM)
    sign = jnp.concatenate([-jnp.ones((half,), F32), jnp.ones((half,), F32)]).reshape(1, DIFF_HEAD_DIM)
    pos = jnp.broadcast_to(positions.reshape(m, 1), (m, DIFF_HEAD_DIM))
    tab = jax.ShapeDtypeStruct((m, DIFF_HEAD_DIM), F32)

    def rope_tile(s):
        return jnp.minimum(s, rope_steps - 1)

    row = pl.BlockSpec((1, DIFF_HEAD_DIM), lambda s: (0, 0))
    tab_spec = pl.BlockSpec((tm, DIFF_HEAD_DIM), lambda s: (rope_tile(s), 0))
    return pl.pallas_call(
        functools.partial(_adaln_rope_kernel, rope_steps=rope_steps),
        grid=(depth * col_tiles,),
        in_specs=[
            pl.BlockSpec((bsz, d), lambda s: (0, 0)),
            pl.BlockSpec((None, d, tn), lambda s: (s // col_tiles, 0, s % col_tiles)),
            pl.BlockSpec((None, 1, tn), lambda s: (s // col_tiles, 0, s % col_tiles)),
            tab_spec, row, row,
        ],
        out_specs=[pl.BlockSpec((None, bsz, tn), lambda s: (s // col_tiles, 0, s % col_tiles)), tab_spec, tab_spec],
        out_shape=[jax.ShapeDtypeStruct((depth, bsz, n), F32), tab, tab],
        compiler_params=_params("arbitrary"),
        name="adaln_rope",
    )(c, ada_w, ada_b.reshape(depth, 1, n), pos, inv_freq, sign)


def _pw1_glu_kernel(x_ref, sc_ref, sh_ref, wa_ref, wg_ref, ba_ref, bg_ref, o_ref, wa_bf, wg_bf):
    @pl.when(pl.program_id(1) == 0)
    def _():
        _cast_rows(wa_ref, wa_bf)
        _cast_rows(wg_ref, wg_bf)

    for r in range(0, x_ref.shape[0], SUB_ROWS):
        rows = slice(r, r + SUB_ROWS)
        h = (x_ref[rows, :] * (1.0 + sc_ref[0]) + sh_ref[0]).astype(BF16)
        a = _dot(h, wa_bf[...]) + ba_ref[...]
        g = _dot(h, wg_bf[...]) + bg_ref[...]
        o_ref[rows, :] = a * jax.nn.sigmoid(g)


def _pw1_glu(x, sc, sh, w, b, layer, seq, tm=1024, tn=512):
    m, d = x.shape
    n_half = w.shape[2] // 2
    gate_off = n_half // tn
    b3 = b.reshape(b.shape[0], 1, b.shape[1])
    return pl.pallas_call(
        _pw1_glu_kernel,
        grid=(n_half // tn, m // tm),
        in_specs=[
            pl.BlockSpec((tm, d), lambda j, i: (i, 0)),
            _mod_spec(sc, lambda j, i: i * tm // seq),
            _mod_spec(sh, lambda j, i: i * tm // seq),
            pl.BlockSpec((None, d, tn), lambda j, i: (layer, 0, j)),
            pl.BlockSpec((None, d, tn), lambda j, i: (layer, 0, j + gate_off)),
            pl.BlockSpec((None, 1, tn), lambda j, i: (layer, 0, j)),
            pl.BlockSpec((None, 1, tn), lambda j, i: (layer, 0, j + gate_off)),
        ],
        out_specs=pl.BlockSpec((tm, tn), lambda j, i: (i, j)),
        out_shape=jax.ShapeDtypeStruct((m, n_half), F32),
        scratch_shapes=[pltpu.VMEM((d, tn), BF16), pltpu.VMEM((d, tn), BF16)],
        compiler_params=_params("arbitrary", "arbitrary"),
        name="pw1_glu",
    )(x, sc[0], sh[0], w, w, b3, b3)


def _conv_proj_kernel(cur_ref, halo_ref, wdw_ref, bdw_ref, cg_ref, cb_ref, w2_ref, b2_ref, x_ref, gate_ref,
                      lng_ref, lnb_ref, zero_ref, o_ref, w_bf, buf, y, v_sc, z_sc, *, n_cast, n_tiles,
                      tiles_per_seq, row_chunk):
    tt, d = cur_ref.shape
    groups = d // LANES
    nw = w_bf.shape[2]
    n_chunks = d // nw
    gpc = groups // n_chunks
    s = pl.program_id(0)

    @pl.when(s < n_cast)
    def _():
        chunk_rows = w2_ref.shape[0]
        r = pl.multiple_of(s * chunk_rows, chunk_rows)
        for n in range(n_chunks):
            w_bf[n, pl.ds(r, chunk_rows), :] = w2_ref[:, n * nw:(n + 1) * nw].astype(BF16)

    @pl.when(s == n_cast)
    def _():
        v_sc[...] = jnp.zeros_like(v_sc)

    @pl.when(s >= n_cast)
    def _():
        t = jnp.minimum(s - n_cast, n_tiles - 1)
        first = (t % tiles_per_seq) == 0
        for c in range(groups):
            lanes = slice(c * LANES, (c + 1) * LANES)
            buf[c, CONV_HALO:, :] = cur_ref[:, lanes]
            buf[c, :CONV_HALO, :] = jnp.where(first, 0.0, halo_ref[:, lanes])

        tap0 = CONV_HALO - (CONV_WIDTH - 1)

        slot = 0
        for n in range(n_chunks):
            z_sc[n] = _dot(v_sc[slot], w_bf[n])
            dep = None
            for c in range(n * gpc, (n + 1) * gpc):
                for r in range(0, tt, row_chunk):
                    acc = jnp.zeros((row_chunk, LANES), F32)
                    for k in range(CONV_WIDTH):
                        acc = acc + buf[c, r + tap0 + k:r + tap0 + k + row_chunk, :] * wdw_ref[c, k:k + 1, :]
                    acc = acc + bdw_ref[c]
                    y[c, r:r + row_chunk, :] = acc
                    bits = pltpu.bitcast(acc, jnp.int32)
                    for q in range(0, row_chunk, 8):
                        dep = bits[q:q + 8, :] if dep is None else dep | bits[q:q + 8, :]
            slot = (dep & zero_ref[...])[0, 0]

        total = jnp.zeros((tt, nw), F32)
        for n in range(n_chunks):
            lanes = slice(n * nw, (n + 1) * nw)
            zn = DEEPNORM_ALPHA * x_ref[:, lanes] + gate_ref[0, :, lanes] * (z_sc[n] + b2_ref[:, lanes])
            z_sc[n] = zn
            total = total + zn
        mu = jnp.sum(total, axis=-1, keepdims=True) * (1.0 / d)
        sq = jnp.zeros((tt, nw), F32)
        for n in range(n_chunks):
            zc = z_sc[n] - mu
            sq = sq + zc * zc
        rstd = lax.rsqrt(jnp.sum(sq, axis=-1, keepdims=True) * (1.0 / d) + LN_EPS)
        for n in range(n_chunks):
            lanes = slice(n * nw, (n + 1) * nw)
            o_ref[:, lanes] = (z_sc[n] - mu) * rstd * lng_ref[:, lanes] + lnb_ref[:, lanes]

        total = jnp.zeros((tt, LANES), F32)
        for c in range(groups):
            total = total + y[c]
        mu = jnp.sum(total, axis=-1, keepdims=True) * (1.0 / d)
        sq = jnp.zeros((tt, LANES), F32)
        for c in range(groups):
            yc = y[c] - mu
            sq = sq + yc * yc
        rstd = lax.rsqrt(jnp.sum(sq, axis=-1, keepdims=True) * (1.0 / d) + LN_EPS)
        for c in range(groups):
            lanes = slice(c * LANES, (c + 1) * LANES)
            z = (y[c] - mu) * rstd * cg_ref[:, lanes] + cb_ref[:, lanes]
            v_sc[0, :, lanes] = (z * jax.nn.sigmoid(z)).astype(v_sc.dtype)


def _conv_proj(u, w_dw, b_dw, cln_g, cln_b, w2, b2, x, gate, ln_g, ln_b, w_layer, ln_layer, seq,
               tt=256, row_chunk=64, chunk_rows=256, nw=256):
    m, d = u.shape
    groups = d // LANES
    halo_per_tile = tt // CONV_HALO
    n_tiles = m // tt
    n_cast = d // chunk_rows
    w_g = jnp.pad(w_dw, ((0, CONV_HALO - CONV_WIDTH), (0, 0))).reshape(CONV_HALO, groups, LANES).transpose(1, 0, 2)
    b_g = b_dw.reshape(groups, 1, LANES)

    def conv_tile(s):
        return jnp.clip(s - n_cast, 0, n_tiles - 1)

    def proj_tile(s):
        return jnp.maximum(s - n_cast - 1, 0)

    vec = pl.BlockSpec((1, d), lambda s: (0, 0))
    kern = functools.partial(_conv_proj_kernel, n_cast=n_cast, n_tiles=n_tiles, tiles_per_seq=seq // tt,
                             row_chunk=row_chunk)
    return pl.pallas_call(
        kern,
        grid=(n_cast + n_tiles + 1,),
        in_specs=[
            pl.BlockSpec((tt, d), lambda s: (conv_tile(s), 0)),
            pl.BlockSpec((CONV_HALO, d), lambda s: (jnp.maximum(conv_tile(s) * halo_per_tile - 1, 0), 0)),
            pl.BlockSpec((groups, CONV_HALO, LANES), lambda s: (0, 0, 0)),
            pl.BlockSpec((groups, 1, LANES), lambda s: (0, 0, 0)),
            vec, vec,
            pl.BlockSpec((None, chunk_rows, d), lambda s: (w_layer, jnp.minimum(s, n_cast - 1), 0)),
            pl.BlockSpec((None, 1, d), lambda s: (w_layer, 0, 0)),
            pl.BlockSpec((tt, d), lambda s: (proj_tile(s), 0)),
            _mod_spec(gate, lambda s: proj_tile(s) * tt // seq),
            pl.BlockSpec((None, 1, d), lambda s: (ln_layer, 0, 0)),
            pl.BlockSpec((None, 1, d), lambda s: (ln_layer, 0, 0)),
            pl.BlockSpec((8, LANES), lambda s: (0, 0)),
        ],
        out_specs=pl.BlockSpec((tt, d), lambda s: (proj_tile(s), 0)),
        out_shape=jax.ShapeDtypeStruct((m, d), F32),
        scratch_shapes=[
            pltpu.VMEM((d // nw, d, nw), BF16),
            pltpu.VMEM((groups, CONV_HALO + tt, LANES), F32),
            pltpu.VMEM((groups, tt, LANES), F32),
            pltpu.VMEM((2, tt, d), BF16),
            pltpu.VMEM((d // nw, tt, nw), F32),
        ],
        compiler_params=_params("arbitrary"),
        name="conv_proj",
    )(u, u, w_g, b_g, cln_g.reshape(1, d), cln_b.reshape(1, d), w2, b2.reshape(-1, 1, d), x, gate[0],
      ln_g.reshape(-1, 1, d), ln_b.reshape(-1, 1, d), jnp.zeros((8, LANES), jnp.int32))


def _proj_res_ln_kernel(*refs, has_bias, n_cast):
    if has_bias:
        a_ref, w_ref, bias_ref, x_ref, gate_ref, lng_ref, lnb_ref, o_ref, w_bf = refs
    else:
        a_ref, w_ref, x_ref, gate_ref, lng_ref, lnb_ref, o_ref, w_bf = refs
    s = pl.program_id(0)
    chunk_rows = w_ref.shape[0]

    @pl.when(s < n_cast)
    def _():
        r = pl.multiple_of(s * chunk_rows, chunk_rows)
        w_bf[pl.ds(r, chunk_rows), :] = w_ref[...].astype(BF16)

    @pl.when(s >= n_cast)
    def _():
        w = w_bf[...]
        for r in range(0, a_ref.shape[0], SUB_ROWS):
            rows = slice(r, r + SUB_ROWS)
            y = _dot(a_ref[rows, :], w)
            if has_bias:
                y = y + bias_ref[...]
            z = DEEPNORM_ALPHA * x_ref[rows, :] + gate_ref[0] * y
            o_ref[rows, :] = _layer_norm(z, lng_ref[...], lnb_ref[...])


def _proj_res_ln(a, w, bias, x, gate, ln_g, ln_b, w_layer, ln_layer, seq, tm=512, chunk_rows=256):
    m, k = a.shape
    d = w.shape[2]
    n_cast = k // chunk_rows

    def tile(s):
        return jnp.maximum(s - n_cast, 0)

    in_specs = [
        pl.BlockSpec((tm, k), lambda s: (tile(s), 0)),
        pl.BlockSpec((None, chunk_rows, d), lambda s: (w_layer, jnp.minimum(s, n_cast - 1), 0)),
    ]
    args = [a, w]
    if bias is not None:
        in_specs.append(pl.BlockSpec((None, 1, d), lambda s: (w_layer, 0, 0)))
        args.append(bias.reshape(bias.shape[0], 1, d))
    in_specs += [
        pl.BlockSpec((tm, d), lambda s: (tile(s), 0)),
        _mod_spec(gate, lambda s: tile(s) * tm // seq),
        pl.BlockSpec((None, 1, d), lambda s: (ln_layer, 0, 0)),
        pl.BlockSpec((None, 1, d), lambda s: (ln_layer, 0, 0)),
    ]
    args += [x, gate[0], ln_g.reshape(-1, 1, d), ln_b.reshape(-1, 1, d)]
    return pl.pallas_call(
        functools.partial(_proj_res_ln_kernel, has_bias=bias is not None, n_cast=n_cast),
        grid=(n_cast + m // tm,),
        in_specs=in_specs,
        out_specs=pl.BlockSpec((tm, d), lambda s: (tile(s), 0)),
        out_shape=jax.ShapeDtypeStruct((m, d), F32),
        scratch_shapes=[pltpu.VMEM((k, d), BF16)],
        compiler_params=_params("arbitrary"),
        name="proj_res_ln",
    )(*args)


def _mlp_kernel(x_ref, sc_ref, sh_ref, gate_ref, w1_ref, b1_ref, w2_ref, b2_ref, lng_ref, lnb_ref, o_ref, h_bf):
    f = pl.program_id(1)
    last = pl.num_programs(1) - 1

    def chunk(h, w1, w2):
        u = _dot(h, w1) + b1_ref[...]
        return _dot(jnp.square(jnp.maximum(u, 0.0)).astype(BF16), w2)

    @pl.when(f == 0)
    def _():
        w1, w2 = w1_ref[...].astype(BF16), w2_ref[...].astype(BF16)
        for r in range(0, x_ref.shape[0], SUB_ROWS):
            rows = slice(r, r + SUB_ROWS)
            h = (x_ref[rows, :] * (1.0 + sc_ref[0]) + sh_ref[0]).astype(BF16)
            h_bf[rows, :] = h
            o_ref[rows, :] = chunk(h, w1, w2)

    @pl.when(jnp.logical_and(f > 0, f < last))
    def _():
        o_ref[...] += chunk(h_bf[...], w1_ref[...].astype(BF16), w2_ref[...].astype(BF16))

    @pl.when(f == last)
    def _():
        w1, w2 = w1_ref[...].astype(BF16), w2_ref[...].astype(BF16)
        for r in range(0, x_ref.shape[0], SUB_ROWS):
            rows = slice(r, r + SUB_ROWS)
            y = o_ref[rows, :] + chunk(h_bf[rows, :], w1, w2) + b2_ref[...]
            z = DEEPNORM_ALPHA * x_ref[rows, :] + gate_ref[0] * y
            o_ref[rows, :] = _layer_norm(z, lng_ref[...], lnb_ref[...])


def _mlp(x, sc, sh, gate, w1, b1, w2, b2, ln_g, ln_b, layer, seq, tm=1024, tf=512):
    m, d = x.shape
    dff = w1.shape[2]
    sc_spec, sh_spec, gate_spec = [_mod_spec(v, lambda i, f: i * tm // seq) for v in (sc, sh, gate)]
    vec_spec = pl.BlockSpec((None, 1, d), lambda i, f: (layer, 0, 0))
    return pl.pallas_call(
        _mlp_kernel,
        grid=(m // tm, dff // tf),
        in_specs=[
            pl.BlockSpec((tm, d), lambda i, f: (i, 0), pipeline_mode=pl.Buffered(1)),
            sc_spec, sh_spec, gate_spec,
            pl.BlockSpec((None, d, tf), lambda i, f: (layer, 0, f)),
            pl.BlockSpec((None, 1, tf), lambda i, f: (layer, 0, f)),
            pl.BlockSpec((None, tf, d), lambda i, f: (layer, f, 0)),
            vec_spec, vec_spec, vec_spec,
        ],
        out_specs=pl.BlockSpec((tm, d), lambda i, f: (i, 0)),
        out_shape=jax.ShapeDtypeStruct((m, d), F32),
        scratch_shapes=[pltpu.VMEM((tm, d), BF16)],
        compiler_params=_params("arbitrary", "arbitrary"),
        name="mlp",
    )(x, sc[0], sh[0], gate[0], w1, b1.reshape(-1, 1, dff), w2, b2.reshape(-1, 1, d),
      ln_g.reshape(-1, 1, d), ln_b.reshape(-1, 1, d))


def _qkv_rope_kernel(x_ref, sc_ref, sh_ref, w_ref, cos_ref, sin_ref, o_ref, w_bf, *, q_tiles, rope_tiles):
    j = pl.program_id(0)

    @pl.when(pl.program_id(1) == 0)
    def _():
        _cast_rows(w_ref, w_bf)

    tn = o_ref.shape[1]

    def step(rope):
        def run():
            scale = jnp.where(j < q_tiles, DIFF_HEAD_DIM ** -0.5 * LOG2_E, 1.0)
            for r in range(0, x_ref.shape[0], SUB_ROWS):
                rows = slice(r, r + SUB_ROWS)
                h = (x_ref[rows, :] * (1.0 + sc_ref[0]) + sh_ref[0]).astype(BF16)
                y = _dot(h, w_bf[...])
                if not rope:
                    o_ref[rows, :] = y.astype(o_ref.dtype)
                    continue
                cos = cos_ref[rows, :] * scale
                sin = sin_ref[rows, :] * scale
                for c in range(tn // DIFF_HEAD_DIM):
                    lanes = slice(c * DIFF_HEAD_DIM, (c + 1) * DIFF_HEAD_DIM)
                    t = y[:, lanes]
                    o_ref[rows, lanes] = (t * cos + pltpu.roll(t, DIFF_HEAD_DIM // 2, 1) * sin).astype(o_ref.dtype)

        return run

    pl.when(j < rope_tiles)(step(True))
    pl.when(j >= rope_tiles)(step(False))


def _qkv_rope(x, sc, sh, w, cos, sin, layer, seq, tm=1024, tn=1024):
    m, d = x.shape
    n = w.shape[2]
    kern = functools.partial(_qkv_rope_kernel, q_tiles=D_MODEL // tn, rope_tiles=2 * D_MODEL // tn)
    return pl.pallas_call(
        kern,
        grid=(n // tn, m // tm),
        in_specs=[
            pl.BlockSpec((tm, d), lambda j, i: (i, 0)),
            _mod_spec(sc, lambda j, i: i * tm // seq),
            _mod_spec(sh, lambda j, i: i * tm // seq),
            pl.BlockSpec((None, d, tn), lambda j, i: (layer, 0, j)),
            pl.BlockSpec((tm, DIFF_HEAD_DIM), lambda j, i: (i, 0)),
            pl.BlockSpec((tm, DIFF_HEAD_DIM), lambda j, i: (i, 0)),
        ],
        out_specs=pl.BlockSpec((tm, tn), lambda j, i: (i, j)),
        out_shape=jax.ShapeDtypeStruct((m, n), BF16),
        scratch_shapes=[pltpu.VMEM((d, tn), BF16)],
        compiler_params=_params("arbitrary", "arbitrary"),
        name="qkv_rope",
    )(x, sc[0], sh[0], w, cos, sin)


def _diff_attn_kernel(q_ref, k_ref, v_ref, lq1_ref, lk1_ref, lq2_ref, lk2_ref, g_ref, o_ref, *, lambda_init):
    tq = q_ref.shape[0]
    dh = DIFF_HEAD_DIM
    vd = DIFF_V_DIM
    lam = (jnp.exp(jnp.sum(lq1_ref[...] * lk1_ref[...], axis=-1, keepdims=True))
           - jnp.exp(jnp.sum(lq2_ref[...] * lk2_ref[...], axis=-1, keepdims=True)) + lambda_init)

    def head_stages(e, n_chunks, row, col):
        st = {"s": [[None] * n_chunks, [None] * n_chunks], "p": [[None] * n_chunks, [None] * n_chunks],
              "mx": [None, None], "sum": [None, None], "coef": [None, None], "acc": None}

        def scores(c):
            def step():
                for mp in range(2):
                    lanes = slice(e * vd + mp * dh, e * vd + (mp + 1) * dh)
                    s = lax.dot_general(q_ref[:, lanes], k_ref[c * tq:(c + 1) * tq, lanes], (((1,), (1,)), ((), ())),
                                        preferred_element_type=F32)
                    if c == n_chunks - 1:
                        s = jnp.where(row >= col, s, -jnp.inf)
                    st["s"][mp][c] = s
                    part = jnp.maximum(s[:, :LANES], s[:, LANES:])
                    st["mx"][mp] = part if c == 0 else jnp.maximum(st["mx"][mp], part)
            return step

        def row_max():
            for mp in range(2):
                st["mx"][mp] = jnp.max(st["mx"][mp], axis=-1, keepdims=True)

        def exps(c):
            def step():
                for mp in range(2):
                    p = jnp.exp2(st["s"][mp][c] - st["mx"][mp])
                    st["p"][mp][c] = p
                    part = p[:, :LANES] + p[:, LANES:]
                    st["sum"][mp] = part if c == 0 else st["sum"][mp] + part
            return step

        def coefs():
            l1, l2 = [jnp.sum(st["sum"][mp], axis=-1, keepdims=True) for mp in range(2)]
            st["coef"] = [1.0 / l1, lam * l1 / l2]

        def values(c):
            def step():
                a = st["p"][0][c] - st["p"][1][c] * st["coef"][1]
                o = _dot(a.astype(v_ref.dtype), v_ref[c * tq:(c + 1) * tq, e * vd:(e + 1) * vd])
                st["acc"] = o if c == 0 else st["acc"] + o
            return step

        def finish():
            o = st["acc"] * st["coef"][0]
            o = o * lax.rsqrt(jnp.mean(o * o, axis=-1, keepdims=True) + RMS_EPS)
            o = o * g_ref[...] * (1.0 - lambda_init)
            o_ref[:, e * vd:(e + 1) * vd] = o.astype(o_ref.dtype)

        chunks = range(n_chunks)
        return ([scores(c) for c in chunks],
                [row_max] + [exps(c) for c in chunks] + [coefs],
                [values(c) for c in chunks] + [finish])

    def tile(n_chunks):
        def run():
            row = lax.broadcasted_iota(jnp.int32, (tq, tq), 0)
            col = lax.broadcasted_iota(jnp.int32, (tq, tq), 1)
            heads = [head_stages(e, n_chunks, row, col) for e in range(q_ref.shape[1] // vd)]
            for t in range(len(heads) + 2):
                live = [heads[t - k][k] for k in range(3) if 0 <= t - k < len(heads)]
                for j in range(max(len(steps) for steps in live)):
                    for steps in live:
                        if j < len(steps):
                            steps[j]()

        return run

    i = pl.program_id(2)
    for n_chunks in range(1, k_ref.shape[0] // tq + 1):
        pl.when(i == n_chunks - 1)(tile(n_chunks))


def _diff_attn(qkv, lq1, lk1, lq2, lk2, subln_g, layer, bsz, seq, lambda_init, tq=256, heads_per_step=4):
    m = qkv.shape[0]
    nq = seq // tq
    vd = DIFF_V_DIM
    gw = heads_per_step * vd
    groups = DIFF_HEADS // heads_per_step
    lam_spec = pl.BlockSpec((None, 1, DIFF_HEAD_DIM), lambda b, h, i: (layer, 0, 0))
    lam_args = [a.reshape(-1, 1, DIFF_HEAD_DIM) for a in (lq1, lk1, lq2, lk2)]
    return pl.pallas_call(
        functools.partial(_diff_attn_kernel, lambda_init=lambda_init),
        grid=(bsz, groups, nq),
        in_specs=[
            pl.BlockSpec((tq, gw), lambda b, h, i: (b * nq + i, h)),
            pl.BlockSpec((seq, gw), lambda b, h, i: (b, groups + h)),
            pl.BlockSpec((seq, gw), lambda b, h, i: (b, 2 * groups + h)),
            lam_spec, lam_spec, lam_spec, lam_spec,
            pl.BlockSpec((None, 1, vd), lambda b, h, i: (layer, 0, 0)),
        ],
        out_specs=pl.BlockSpec((tq, gw), lambda b, h, i: (b * nq + i, h)),
        out_shape=jax.ShapeDtypeStruct((m, DIFF_HEADS * vd), BF16),
        compiler_params=_params("arbitrary", "arbitrary", "arbitrary"),
        name="diff_attn",
    )(qkv, qkv, qkv, *lam_args, subln_g.reshape(-1, 1, vd))


def kernel(x, c, positions, ada_w, ada_b, ln_mix_g, ln_mix_b, ln_ffn_g, ln_ffn_b, conv_pw1_w, conv_pw1_b, conv_dw_w, conv_dw_b, conv_ln_g, conv_ln_b, conv_pw2_w, conv_pw2_b, attn_qkv_w, attn_lq1, attn_lk1, attn_lq2, attn_lk2, attn_subln_g, attn_o_w, mlp_w1, mlp_b1, mlp_w2, mlp_b2):
    bsz, seq, d = x.shape
    xf = x.reshape(bsz * seq, d)
    mod, cos, sin = _adaln_rope(c, ada_w, ada_b, positions)
    mod = mod.reshape(DEPTH, bsz, 6, 1, d)
    for i in range(DEPTH):
        sh_m, sc_m, g_m, sh_f, sc_f, g_f = [(mod, i, k) for k in range(6)]
        j = i // N_MIXERS
        if i % N_MIXERS == 0:
            u = _pw1_glu(xf, sc_m, sh_m, conv_pw1_w, conv_pw1_b, j, seq)
            xf = _conv_proj(u, conv_dw_w[j], conv_dw_b[j], conv_ln_g[j], conv_ln_b[j], conv_pw2_w, conv_pw2_b,
                            xf, g_m, ln_mix_g, ln_mix_b, j, i, seq)
        else:
            lambda_init = 0.8 - 0.6 * math.exp(-0.3 * i)
            qkv = _qkv_rope(xf, sc_m, sh_m, attn_qkv_w, cos, sin, j, seq)
            o = _diff_attn(qkv, attn_lq1, attn_lk1, attn_lq2, attn_lk2, attn_subln_g, j, bsz, seq, lambda_init)
            xf = _proj_res_ln(o, attn_o_w, None, xf, g_m, ln_mix_g, ln_mix_b, j, i, seq)
        xf = _mlp(xf, sc_f, sh_f, g_f, mlp_w1, mlp_b1, mlp_w2, mlp_b2, ln_ffn_g, ln_ffn_b, i, seq)
    return xf.reshape(bsz, seq, d)
```

```python
import functools
import math

import jax
import jax.numpy as jnp
from jax import lax
from jax.experimental import pallas as pl
from jax.experimental.pallas import tpu as pltpu

F32 = jnp.float32
BF16 = jnp.bfloat16

D_MODEL = 2048
DEPTH = 2
N_MIXERS = 2
CONV_WIDTH = 31
DIFF_HEADS = 8
DIFF_HEAD_DIM = D_MODEL // (2 * DIFF_HEADS)
DIFF_V_DIM = 2 * DIFF_HEAD_DIM
D_FF = 4 * D_MODEL
ROPE_THETA = 10000.0
DEEPNORM_ALPHA = (2.0 * DEPTH) ** 0.25
LN_EPS = 1e-5
RMS_EPS = 1e-5
LOG2_E = math.log2(math.e)

LANES = 128
SUB_ROWS = 256
CONV_HALO = 32
VMEM_LIMIT = 60 * 1024 * 1024


def _dot(a, b):
    return jnp.dot(a, b, preferred_element_type=F32)


def _layer_norm(z, g, b):
    mu = jnp.mean(z, axis=-1, keepdims=True)
    zc = z - mu
    var = jnp.mean(zc * zc, axis=-1, keepdims=True)
    return zc * lax.rsqrt(var + LN_EPS) * g + b


def _cast_rows(src_ref, dst_ref, rows_per_step=256):
    n = src_ref.shape[0] // rows_per_step

    def body(i, carry):
        r = pl.multiple_of(i * rows_per_step, rows_per_step)
        dst_ref[pl.ds(r, rows_per_step), :] = src_ref[pl.ds(r, rows_per_step), :].astype(dst_ref.dtype)
        return carry

    lax.fori_loop(0, n, body, 0)


def _params(*sem):
    return pltpu.CompilerParams(dimension_semantics=sem, vmem_limit_bytes=VMEM_LIMIT)


def _mod_spec(mod, batch_of):
    arr, layer, comp = mod
    return pl.BlockSpec((None, None, 1, 1, arr.shape[-1]), lambda *g: (layer, batch_of(*g), comp, 0, 0))


def _adaln_rope_kernel(c_ref, w_ref, b_ref, pos_ref, inv_freq_ref, sign_ref, o_ref, cos_ref, sin_ref, *,
                       rope_steps, col_tiles):
    c = c_ref[...]
    cond = (c * jax.nn.sigmoid(c)).astype(BF16)
    o_ref[...] = _dot(cond, w_ref[...].astype(BF16)) + b_ref[pl.ds(pl.program_id(0) // col_tiles, 1), :]

    @pl.when(pl.program_id(0) < rope_steps)
    def _():
        ang = pos_ref[...].astype(F32) * inv_freq_ref[...]
        cos_ref[...] = jnp.cos(ang)
        sin_ref[...] = jnp.sin(ang) * sign_ref[...]


def _adaln_rope(c, ada_w, ada_b, positions, tn=1024, tm=1024):
    depth, d, n = ada_w.shape
    bsz = c.shape[0]
    col_tiles = n // tn
    m = positions.size
    rope_steps = m // tm
    assert rope_steps <= depth * col_tiles
    half = DIFF_HEAD_DIM // 2
    inv_freq = ROPE_THETA ** (-jnp.arange(0, DIFF_HEAD_DIM, 2, dtype=F32) / DIFF_HEAD_DIM)
    inv_freq = jnp.concatenate([inv_freq, inv_freq]).reshape(1, DIFF_HEAD_DIM)
    sign = jnp.concatenate([-jnp.ones((half,), F32), jnp.ones((half,), F32)]).reshape(1, DIFF_HEAD_DIM)
    pos = jnp.broadcast_to(positions.reshape(m, 1), (m, DIFF_HEAD_DIM))
    tab = jax.ShapeDtypeStruct((m, DIFF_HEAD_DIM), F32)

    def rope_tile(s):
        return jnp.minimum(s, rope_steps - 1)

    row = pl.BlockSpec((1, DIFF_HEAD_DIM), lambda s: (0, 0))
    tab_spec = pl.BlockSpec((tm, DIFF_HEAD_DIM), lambda s: (rope_tile(s), 0))
    return pl.pallas_call(
        functools.partial(_adaln_rope_kernel, rope_steps=rope_steps, col_tiles=col_tiles),
        grid=(depth * col_tiles,),
        in_specs=[
            pl.BlockSpec((bsz, d), lambda s: (0, 0)),
            pl.BlockSpec((None, d, tn), lambda s: (s // col_tiles, 0, s % col_tiles)),
            pl.BlockSpec((depth, tn), lambda s: (0, s % col_tiles)),
            tab_spec, row, row,
        ],
        out_specs=[pl.BlockSpec((None, bsz, tn), lambda s: (s // col_tiles, 0, s % col_tiles)), tab_spec, tab_spec],
        out_shape=[jax.ShapeDtypeStruct((depth, bsz, n), F32), tab, tab],
        compiler_params=_params("arbitrary"),
        name="adaln_rope",
    )(c, ada_w, ada_b, pos, inv_freq, sign)


def _pw1_glu_kernel(x_ref, sc_ref, sh_ref, wa_ref, wg_ref, ba_ref, bg_ref, o_ref, wa_bf, wg_bf, *, layer):
    @pl.when(pl.program_id(1) == 0)
    def _():
        _cast_rows(wa_ref, wa_bf)
        _cast_rows(wg_ref, wg_bf)

    for r in range(0, x_ref.shape[0], SUB_ROWS):
        rows = slice(r, r + SUB_ROWS)
        h = (x_ref[rows, :] * (1.0 + sc_ref[0]) + sh_ref[0]).astype(BF16)
        a = _dot(h, wa_bf[...]) + ba_ref[layer:layer + 1, :]
        g = _dot(h, wg_bf[...]) + bg_ref[layer:layer + 1, :]
        o_ref[rows, :] = a * jax.nn.sigmoid(g)


def _pw1_glu(x, sc, sh, w, b, layer, seq, tm=1024, tn=512):
    m, d = x.shape
    n_half = w.shape[2] // 2
    gate_off = n_half // tn
    return pl.pallas_call(
        functools.partial(_pw1_glu_kernel, layer=layer),
        grid=(n_half // tn, m // tm),
        in_specs=[
            pl.BlockSpec((tm, d), lambda j, i: (i, 0)),
            _mod_spec(sc, lambda j, i: i * tm // seq),
            _mod_spec(sh, lambda j, i: i * tm // seq),
            pl.BlockSpec((None, d, tn), lambda j, i: (layer, 0, j)),
            pl.BlockSpec((None, d, tn), lambda j, i: (layer, 0, j + gate_off)),
            pl.BlockSpec((b.shape[0], tn), lambda j, i: (0, j)),
            pl.BlockSpec((b.shape[0], tn), lambda j, i: (0, j + gate_off)),
        ],
        out_specs=pl.BlockSpec((tm, tn), lambda j, i: (i, j)),
        out_shape=jax.ShapeDtypeStruct((m, n_half), F32),
        scratch_shapes=[pltpu.VMEM((d, tn), BF16), pltpu.VMEM((d, tn), BF16)],
        compiler_params=_params("arbitrary", "arbitrary"),
        name="pw1_glu",
    )(x, sc[0], sh[0], w, w, b, b)


def _conv_proj_kernel(cur_ref, halo_ref, wdw_ref, bdw_ref, cg_ref, cb_ref, w2_ref, b2_ref, x_ref, gate_ref,
                      lng_ref, lnb_ref, zero_ref, o_ref, w_bf, buf, y, v_sc, z_sc, *, n_cast, n_tiles,
                      tiles_per_seq, row_chunk, w_layer, ln_layer):
    tt, d = cur_ref.shape
    groups = d // LANES
    nw = w_bf.shape[2]
    n_chunks = d // nw
    gpc = groups // n_chunks
    s = pl.program_id(0)

    @pl.when(s < n_cast)
    def _():
        chunk_rows = w2_ref.shape[0]
        r = pl.multiple_of(s * chunk_rows, chunk_rows)
        for n in range(n_chunks):
            w_bf[n, pl.ds(r, chunk_rows), :] = w2_ref[:, n * nw:(n + 1) * nw].astype(BF16)

    @pl.when(s == n_cast)
    def _():
        v_sc[...] = jnp.zeros_like(v_sc)

    @pl.when(s >= n_cast)
    def _():
        t = jnp.minimum(s - n_cast, n_tiles - 1)
        first = (t % tiles_per_seq) == 0
        for c in range(groups):
            lanes = slice(c * LANES, (c + 1) * LANES)
            buf[c, CONV_HALO:, :] = cur_ref[:, lanes]
            buf[c, :CONV_HALO, :] = jnp.where(first, 0.0, halo_ref[:, lanes])

        tap0 = CONV_HALO - (CONV_WIDTH - 1)

        slot = 0
        for n in range(n_chunks):
            z_sc[n] = _dot(v_sc[slot], w_bf[n])
            dep = None
            for c in range(n * gpc, (n + 1) * gpc):
                for r in range(0, tt, row_chunk):
                    acc = jnp.zeros((row_chunk, LANES), F32)
                    for k in range(CONV_WIDTH):
                        acc = acc + buf[c, r + tap0 + k:r + tap0 + k + row_chunk, :] * wdw_ref[c, k:k + 1, :]
                    acc = acc + bdw_ref[c]
                    y[c, r:r + row_chunk, :] = acc
                    bits = pltpu.bitcast(acc, jnp.int32)
                    for q in range(0, row_chunk, 8):
                        dep = bits[q:q + 8, :] if dep is None else dep | bits[q:q + 8, :]
            slot = (dep & zero_ref[...])[0, 0]

        total = jnp.zeros((tt, nw), F32)
        for n in range(n_chunks):
            lanes = slice(n * nw, (n + 1) * nw)
            zn = DEEPNORM_ALPHA * x_ref[:, lanes] + gate_ref[0, :, lanes] * (z_sc[n] + b2_ref[w_layer:w_layer + 1, lanes])
            z_sc[n] = zn
            total = total + zn
        mu = jnp.sum(total, axis=-1, keepdims=True) * (1.0 / d)
        sq = jnp.zeros((tt, nw), F32)
        for n in range(n_chunks):
            zc = z_sc[n] - mu
            sq = sq + zc * zc
        rstd = lax.rsqrt(jnp.sum(sq, axis=-1, keepdims=True) * (1.0 / d) + LN_EPS)
        for n in range(n_chunks):
            lanes = slice(n * nw, (n + 1) * nw)
            o_ref[:, lanes] = ((z_sc[n] - mu) * rstd * lng_ref[ln_layer:ln_layer + 1, lanes]
                               + lnb_ref[ln_layer:ln_layer + 1, lanes])

        total = jnp.zeros((tt, LANES), F32)
        for c in range(groups):
            total = total + y[c]
        mu = jnp.sum(total, axis=-1, keepdims=True) * (1.0 / d)
        sq = jnp.zeros((tt, LANES), F32)
        for c in range(groups):
            yc = y[c] - mu
            sq = sq + yc * yc
        rstd = lax.rsqrt(jnp.sum(sq, axis=-1, keepdims=True) * (1.0 / d) + LN_EPS)
        for c in range(groups):
            lanes = slice(c * LANES, (c + 1) * LANES)
            z = (y[c] - mu) * rstd * cg_ref[:, lanes] + cb_ref[:, lanes]
            v_sc[0, :, lanes] = (z * jax.nn.sigmoid(z)).astype(v_sc.dtype)


def _conv_proj(u, w_dw, b_dw, cln_g, cln_b, w2, b2, x, gate, ln_g, ln_b, w_layer, ln_layer, seq,
               tt=256, row_chunk=64, chunk_rows=256, nw=256):
    m, d = u.shape
    groups = d // LANES
    halo_per_tile = tt // CONV_HALO
    n_tiles = m // tt
    n_cast = d // chunk_rows
    w_g = jnp.pad(w_dw, ((0, CONV_HALO - CONV_WIDTH), (0, 0))).reshape(CONV_HALO, groups, LANES).transpose(1, 0, 2)
    b_g = b_dw.reshape(groups, 1, LANES)

    def conv_tile(s):
        return jnp.clip(s - n_cast, 0, n_tiles - 1)

    def proj_tile(s):
        return jnp.maximum(s - n_cast - 1, 0)

    vec = pl.BlockSpec((1, d), lambda s: (0, 0))
    kern = functools.partial(_conv_proj_kernel, n_cast=n_cast, n_tiles=n_tiles, tiles_per_seq=seq // tt,
                             row_chunk=row_chunk, w_layer=w_layer, ln_layer=ln_layer)
    return pl.pallas_call(
        kern,
        grid=(n_cast + n_tiles + 1,),
        in_specs=[
            pl.BlockSpec((tt, d), lambda s: (conv_tile(s), 0)),
            pl.BlockSpec((CONV_HALO, d), lambda s: (jnp.maximum(conv_tile(s) * halo_per_tile - 1, 0), 0)),
            pl.BlockSpec((groups, CONV_HALO, LANES), lambda s: (0, 0, 0)),
            pl.BlockSpec((groups, 1, LANES), lambda s: (0, 0, 0)),
            vec, vec,
            pl.BlockSpec((None, chunk_rows, d), lambda s: (w_layer, jnp.minimum(s, n_cast - 1), 0)),
            pl.BlockSpec(b2.shape, lambda s: (0, 0)),
            pl.BlockSpec((tt, d), lambda s: (proj_tile(s), 0)),
            _mod_spec(gate, lambda s: proj_tile(s) * tt // seq),
            pl.BlockSpec(ln_g.shape, lambda s: (0, 0)),
            pl.BlockSpec(ln_b.shape, lambda s: (0, 0)),
            pl.BlockSpec((8, LANES), lambda s: (0, 0)),
        ],
        out_specs=pl.BlockSpec((tt, d), lambda s: (proj_tile(s), 0)),
        out_shape=jax.ShapeDtypeStruct((m, d), F32),
        scratch_shapes=[
            pltpu.VMEM((d // nw, d, nw), BF16),
            pltpu.VMEM((groups, CONV_HALO + tt, LANES), F32),
            pltpu.VMEM((groups, tt, LANES), F32),
            pltpu.VMEM((2, tt, d), BF16),
            pltpu.VMEM((d // nw, tt, nw), F32),
        ],
        compiler_params=_params("arbitrary"),
        name="conv_proj",
    )(u, u, w_g, b_g, cln_g.reshape(1, d), cln_b.reshape(1, d), w2, b2, x, gate[0],
      ln_g, ln_b, jnp.zeros((8, LANES), jnp.int32))


def _proj_res_ln_kernel(*refs, has_bias, n_cast, w_layer, ln_layer):
    if has_bias:
        a_ref, w_ref, bias_ref, x_ref, gate_ref, lng_ref, lnb_ref, o_ref, w_bf = refs
    else:
        a_ref, w_ref, x_ref, gate_ref, lng_ref, lnb_ref, o_ref, w_bf = refs
    s = pl.program_id(0)
    chunk_rows = w_ref.shape[0]

    @pl.when(s < n_cast)
    def _():
        r = pl.multiple_of(s * chunk_rows, chunk_rows)
        w_bf[pl.ds(r, chunk_rows), :] = w_ref[...].astype(BF16)

    @pl.when(s >= n_cast)
    def _():
        w = w_bf[...]
        for r in range(0, a_ref.shape[0], SUB_ROWS):
            rows = slice(r, r + SUB_ROWS)
            y = _dot(a_ref[rows, :], w)
            if has_bias:
                y = y + bias_ref[w_layer:w_layer + 1, :]
            z = DEEPNORM_ALPHA * x_ref[rows, :] + gate_ref[0] * y
            o_ref[rows, :] = _layer_norm(z, lng_ref[ln_layer:ln_layer + 1, :], lnb_ref[ln_layer:ln_layer + 1, :])


def _proj_res_ln(a, w, bias, x, gate, ln_g, ln_b, w_layer, ln_layer, seq, tm=512, chunk_rows=256):
    m, k = a.shape
    d = w.shape[2]
    n_cast = k // chunk_rows

    def tile(s):
        return jnp.maximum(s - n_cast, 0)

    in_specs = [
        pl.BlockSpec((tm, k), lambda s: (tile(s), 0)),
        pl.BlockSpec((None, chunk_rows, d), lambda s: (w_layer, jnp.minimum(s, n_cast - 1), 0)),
    ]
    args = [a, w]
    if bias is not None:
        in_specs.append(pl.BlockSpec(bias.shape, lambda s: (0, 0)))
        args.append(bias)
    in_specs += [
        pl.BlockSpec((tm, d), lambda s: (tile(s), 0)),
        _mod_spec(gate, lambda s: tile(s) * tm // seq),
        pl.BlockSpec(ln_g.shape, lambda s: (0, 0)),
        pl.BlockSpec(ln_b.shape, lambda s: (0, 0)),
    ]
    args += [x, gate[0], ln_g, ln_b]
    return pl.pallas_call(
        functools.partial(_proj_res_ln_kernel, has_bias=bias is not None, n_cast=n_cast, w_layer=w_layer,
                          ln_layer=ln_layer),
        grid=(n_cast + m // tm,),
        in_specs=in_specs,
        out_specs=pl.BlockSpec((tm, d), lambda s: (tile(s), 0)),
        out_shape=jax.ShapeDtypeStruct((m, d), F32),
        scratch_shapes=[pltpu.VMEM((k, d), BF16)],
        compiler_params=_params("arbitrary"),
        name="proj_res_ln",
    )(*args)


def _mlp_kernel(x_ref, sc_ref, sh_ref, gate_ref, w1_ref, b1_ref, w2_ref, b2_ref, lng_ref, lnb_ref, o_ref, h_bf, *,
                layer):
    f = pl.program_id(1)
    last = pl.num_programs(1) - 1

    def chunk(h, w1, w2):
        u = _dot(h, w1) + b1_ref[layer:layer + 1, :]
        return _dot(jnp.square(jnp.maximum(u, 0.0)).astype(BF16), w2)

    @pl.when(f == 0)
    def _():
        w1, w2 = w1_ref[...].astype(BF16), w2_ref[...].astype(BF16)
        for r in range(0, x_ref.shape[0], SUB_ROWS):
            rows = slice(r, r + SUB_ROWS)
            h = (x_ref[rows, :] * (1.0 + sc_ref[0]) + sh_ref[0]).astype(BF16)
            h_bf[rows, :] = h
            o_ref[rows, :] = chunk(h, w1, w2)

    @pl.when(jnp.logical_and(f > 0, f < last))
    def _():
        o_ref[...] += chunk(h_bf[...], w1_ref[...].astype(BF16), w2_ref[...].astype(BF16))

    @pl.when(f == last)
    def _():
        w1, w2 = w1_ref[...].astype(BF16), w2_ref[...].astype(BF16)
        for r in range(0, x_ref.shape[0], SUB_ROWS):
            rows = slice(r, r + SUB_ROWS)
            y = o_ref[rows, :] + chunk(h_bf[rows, :], w1, w2) + b2_ref[layer:layer + 1, :]
            z = DEEPNORM_ALPHA * x_ref[rows, :] + gate_ref[0] * y
            o_ref[rows, :] = _layer_norm(z, lng_ref[layer:layer + 1, :], lnb_ref[layer:layer + 1, :])


def _mlp(x, sc, sh, gate, w1, b1, w2, b2, ln_g, ln_b, layer, seq, tm=1024, tf=512):
    m, d = x.shape
    dff = w1.shape[2]
    sc_spec, sh_spec, gate_spec = [_mod_spec(v, lambda i, f: i * tm // seq) for v in (sc, sh, gate)]
    vec_spec = pl.BlockSpec(b2.shape, lambda i, f: (0, 0))
    return pl.pallas_call(
        functools.partial(_mlp_kernel, layer=layer),
        grid=(m // tm, dff // tf),
        in_specs=[
            pl.BlockSpec((tm, d), lambda i, f: (i, 0), pipeline_mode=pl.Buffered(1)),
            sc_spec, sh_spec, gate_spec,
            pl.BlockSpec((None, d, tf), lambda i, f: (layer, 0, f)),
            pl.BlockSpec((b1.shape[0], tf), lambda i, f: (0, f)),
            pl.BlockSpec((None, tf, d), lambda i, f: (layer, f, 0)),
            vec_spec, vec_spec, vec_spec,
        ],
        out_specs=pl.BlockSpec((tm, d), lambda i, f: (i, 0)),
        out_shape=jax.ShapeDtypeStruct((m, d), F32),
        scratch_shapes=[pltpu.VMEM((tm, d), BF16)],
        compiler_params=_params("arbitrary", "arbitrary"),
        name="mlp",
    )(x, sc[0], sh[0], gate[0], w1, b1, w2, b2, ln_g, ln_b)


def _qkv_rope_kernel(x_ref, sc_ref, sh_ref, w_ref, cos_ref, sin_ref, o_ref, w_bf, *, q_tiles, rope_tiles):
    j = pl.program_id(0)

    @pl.when(pl.program_id(1) == 0)
    def _():
        _cast_rows(w_ref, w_bf)

    tn = o_ref.shape[1]

    def step(rope):
        def run():
            scale = jnp.where(j < q_tiles, DIFF_HEAD_DIM ** -0.5 * LOG2_E, 1.0)
            for r in range(0, x_ref.shape[0], SUB_ROWS):
                rows = slice(r, r + SUB_ROWS)
                h = (x_ref[rows, :] * (1.0 + sc_ref[0]) + sh_ref[0]).astype(BF16)
                y = _dot(h, w_bf[...])
                if not rope:
                    o_ref[rows, :] = y.astype(o_ref.dtype)
                    continue
                cos = cos_ref[rows, :] * scale
                sin = sin_ref[rows, :] * scale
                for c in range(tn // DIFF_HEAD_DIM):
                    lanes = slice(c * DIFF_HEAD_DIM, (c + 1) * DIFF_HEAD_DIM)
                    t = y[:, lanes]
                    o_ref[rows, lanes] = (t * cos + pltpu.roll(t, DIFF_HEAD_DIM // 2, 1) * sin).astype(o_ref.dtype)

        return run

    pl.when(j < rope_tiles)(step(True))
    pl.when(j >= rope_tiles)(step(False))


def _qkv_rope(x, sc, sh, w, cos, sin, layer, seq, tm=1024, tn=1024):
    m, d = x.shape
    n = w.shape[2]
    kern = functools.partial(_qkv_rope_kernel, q_tiles=D_MODEL // tn, rope_tiles=2 * D_MODEL // tn)
    return pl.pallas_call(
        kern,
        grid=(n // tn, m // tm),
        in_specs=[
            pl.BlockSpec((tm, d), lambda j, i: (i, 0)),
            _mod_spec(sc, lambda j, i: i * tm // seq),
            _mod_spec(sh, lambda j, i: i * tm // seq),
            pl.BlockSpec((None, d, tn), lambda j, i: (layer, 0, j)),
            pl.BlockSpec((tm, DIFF_HEAD_DIM), lambda j, i: (i, 0)),
            pl.BlockSpec((tm, DIFF_HEAD_DIM), lambda j, i: (i, 0)),
        ],
        out_specs=pl.BlockSpec((tm, tn), lambda j, i: (i, j)),
        out_shape=jax.ShapeDtypeStruct((m, n), BF16),
        scratch_shapes=[pltpu.VMEM((d, tn), BF16)],
        compiler_params=_params("arbitrary", "arbitrary"),
        name="qkv_rope",
    )(x, sc[0], sh[0], w, cos, sin)


def _diff_attn_kernel(q_ref, k_ref, v_ref, lq1_ref, lk1_ref, lq2_ref, lk2_ref, g_ref, o_ref, *, lambda_init, layer):
    tq = q_ref.shape[0]
    dh = DIFF_HEAD_DIM
    vd = DIFF_V_DIM
    layer_row = slice(layer, layer + 1)
    lam = (jnp.exp(jnp.sum(lq1_ref[layer_row, :] * lk1_ref[layer_row, :], axis=-1, keepdims=True))
           - jnp.exp(jnp.sum(lq2_ref[layer_row, :] * lk2_ref[layer_row, :], axis=-1, keepdims=True)) + lambda_init)
    subln_g = g_ref[layer_row, :]

    def head_stages(e, n_chunks, row, col):
        st = {"s": [[None] * n_chunks, [None] * n_chunks], "p": [[None] * n_chunks, [None] * n_chunks],
              "mx": [None, None], "sum": [None, None], "coef": [None, None], "acc": None}

        def scores(c):
            def step():
                for mp in range(2):
                    lanes = slice(e * vd + mp * dh, e * vd + (mp + 1) * dh)
                    s = lax.dot_general(q_ref[:, lanes], k_ref[c * tq:(c + 1) * tq, lanes], (((1,), (1,)), ((), ())),
                                        preferred_element_type=F32)
                    if c == n_chunks - 1:
                        s = jnp.where(row >= col, s, -jnp.inf)
                    st["s"][mp][c] = s
                    part = jnp.maximum(s[:, :LANES], s[:, LANES:])
                    st["mx"][mp] = part if c == 0 else jnp.maximum(st["mx"][mp], part)
            return step

        def row_max():
            for mp in range(2):
                st["mx"][mp] = jnp.max(st["mx"][mp], axis=-1, keepdims=True)

        def exps(c):
            def step():
                for mp in range(2):
                    p = jnp.exp2(st["s"][mp][c] - st["mx"][mp])
                    st["p"][mp][c] = p
                    part = p[:, :LANES] + p[:, LANES:]
                    st["sum"][mp] = part if c == 0 else st["sum"][mp] + part
            return step

        def coefs():
            l1, l2 = [jnp.sum(st["sum"][mp], axis=-1, keepdims=True) for mp in range(2)]
            st["coef"] = [1.0 / l1, lam * l1 / l2]

        def values(c):
            def step():
                a = st["p"][0][c] - st["p"][1][c] * st["coef"][1]
                o = _dot(a.astype(v_ref.dtype), v_ref[c * tq:(c + 1) * tq, e * vd:(e + 1) * vd])
                st["acc"] = o if c == 0 else st["acc"] + o
            return step

        def finish():
            o = st["acc"] * st["coef"][0]
            o = o * lax.rsqrt(jnp.mean(o * o, axis=-1, keepdims=True) + RMS_EPS)
            o = o * subln_g * (1.0 - lambda_init)
            o_ref[:, e * vd:(e + 1) * vd] = o.astype(o_ref.dtype)

        chunks = range(n_chunks)
        return ([scores(c) for c in chunks],
                [row_max] + [exps(c) for c in chunks] + [coefs],
                [values(c) for c in chunks] + [finish])

    def tile(n_chunks):
        def run():
            row = lax.broadcasted_iota(jnp.int32, (tq, tq), 0)
            col = lax.broadcasted_iota(jnp.int32, (tq, tq), 1)
            heads = [head_stages(e, n_chunks, row, col) for e in range(q_ref.shape[1] // vd)]
            for t in range(len(heads) + 2):
                live = [heads[t - k][k] for k in range(3) if 0 <= t - k < len(heads)]
                for j in range(max(len(steps) for steps in live)):
                    for steps in live:
                        if j < len(steps):
                            steps[j]()

        return run

    i = pl.program_id(2)
    for n_chunks in range(1, k_ref.shape[0] // tq + 1):
        pl.when(i == n_chunks - 1)(tile(n_chunks))


def _diff_attn(qkv, lq1, lk1, lq2, lk2, subln_g, layer, bsz, seq, lambda_init, tq=256, heads_per_step=8):
    m = qkv.shape[0]
    nq = seq // tq
    vd = DIFF_V_DIM
    gw = heads_per_step * vd
    groups = DIFF_HEADS // heads_per_step
    lam_spec = pl.BlockSpec(lq1.shape, lambda b, h, i: (0, 0))
    return pl.pallas_call(
        functools.partial(_diff_attn_kernel, lambda_init=lambda_init, layer=layer),
        grid=(bsz, groups, nq),
        in_specs=[
            pl.BlockSpec((tq, gw), lambda b, h, i: (b * nq + i, h)),
            pl.BlockSpec((seq, gw), lambda b, h, i: (b, groups + h)),
            pl.BlockSpec((seq, gw), lambda b, h, i: (b, 2 * groups + h)),
            lam_spec, lam_spec, lam_spec, lam_spec,
            pl.BlockSpec(subln_g.shape, lambda b, h, i: (0, 0)),
        ],
        out_specs=pl.BlockSpec((tq, gw), lambda b, h, i: (b * nq + i, h)),
        out_shape=jax.ShapeDtypeStruct((m, DIFF_HEADS * vd), BF16),
        compiler_params=_params("arbitrary", "arbitrary", "arbitrary"),
        name="diff_attn",
    )(qkv, qkv, qkv, lq1, lk1, lq2, lk2, subln_g)


def kernel(x, c, positions, ada_w, ada_b, ln_mix_g, ln_mix_b, ln_ffn_g, ln_ffn_b, conv_pw1_w, conv_pw1_b, conv_dw_w, conv_dw_b, conv_ln_g, conv_ln_b, conv_pw2_w, conv_pw2_b, attn_qkv_w, attn_lq1, attn_lk1, attn_lq2, attn_lk2, attn_subln_g, attn_o_w, mlp_w1, mlp_b1, mlp_w2, mlp_b2):
    bsz, seq, d = x.shape
    xf = x.reshape(bsz * seq, d)
    mod, cos, sin = _adaln_rope(c, ada_w, ada_b, positions)
    mod = mod.reshape(DEPTH, bsz, 6, 1, d)
    for i in range(DEPTH):
        sh_m, sc_m, g_m, sh_f, sc_f, g_f = [(mod, i, k) for k in range(6)]
        j = i // N_MIXERS
        if i % N_MIXERS == 0:
            u = _pw1_glu(xf, sc_m, sh_m, conv_pw1_w, conv_pw1_b, j, seq)
            xf = _conv_proj(u, conv_dw_w[j], conv_dw_b[j], conv_ln_g[j], conv_ln_b[j], conv_pw2_w, conv_pw2_b,
                            xf, g_m, ln_mix_g, ln_mix_b, j, i, seq)
        else:
            lambda_init = 0.8 - 0.6 * math.exp(-0.3 * i)
            qkv = _qkv_rope(xf, sc_m, sh_m, attn_qkv_w, cos, sin, j, seq)
            o = _diff_attn(qkv, attn_lq1, attn_lk1, attn_lq2, attn_lk2, attn_subln_g, j, bsz, seq, lambda_init)
            xf = _proj_res_ln(o, attn_o_w, None, xf, g_m, ln_mix_g, ln_mix_b, j, i, seq)
        xf = _mlp(xf, sc_f, sh_f, g_f, mlp_w1, mlp_b1, mlp_w2, mlp_b2, ln_ffn_g, ln_ffn_b, i, seq)
    return xf.reshape(bsz, seq, d)
```

```python
import functools
import math

import jax
import jax.numpy as jnp
from jax import lax
from jax.experimental import pallas as pl
from jax.experimental.pallas import tpu as pltpu

F32 = jnp.float32
BF16 = jnp.bfloat16

D_MODEL = 2048
DEPTH = 2
N_MIXERS = 2
CONV_WIDTH = 31
DIFF_HEADS = 8
DIFF_HEAD_DIM = D_MODEL // (2 * DIFF_HEADS)
DIFF_V_DIM = 2 * DIFF_HEAD_DIM
D_FF = 4 * D_MODEL
ROPE_THETA = 10000.0
DEEPNORM_ALPHA = (2.0 * DEPTH) ** 0.25
LN_EPS = 1e-5
RMS_EPS = 1e-5
LOG2_E = math.log2(math.e)

LANES = 128
SUB_ROWS = 256
CONV_HALO = 32
VMEM_LIMIT = 60 * 1024 * 1024


def _dot(a, b):
    return jnp.dot(a, b, preferred_element_type=F32)


def _layer_norm(z, g, b):
    mu = jnp.mean(z, axis=-1, keepdims=True)
    zc = z - mu
    var = jnp.mean(zc * zc, axis=-1, keepdims=True)
    return zc * lax.rsqrt(var + LN_EPS) * g + b


def _cast_rows(src_ref, dst_ref, rows_per_step=256):
    n = src_ref.shape[0] // rows_per_step

    def body(i, carry):
        r = pl.multiple_of(i * rows_per_step, rows_per_step)
        dst_ref[pl.ds(r, rows_per_step), :] = src_ref[pl.ds(r, rows_per_step), :].astype(dst_ref.dtype)
        return carry

    lax.fori_loop(0, n, body, 0)


def _params(*sem):
    return pltpu.CompilerParams(dimension_semantics=sem, vmem_limit_bytes=VMEM_LIMIT)


def _mod_spec(mod, batch_of):
    arr, layer, comp = mod
    return pl.BlockSpec((None, None, 1, 1, arr.shape[-1]), lambda *g: (layer, batch_of(*g), comp, 0, 0))


def _adaln_rope_kernel(c_ref, w_ref, b_ref, pos_ref, inv_freq_ref, sign_ref, o_ref, cos_ref, sin_ref, *,
                       rope_steps, col_tiles):
    c = c_ref[...]
    cond = (c * jax.nn.sigmoid(c)).astype(BF16)
    o_ref[...] = _dot(cond, w_ref[...].astype(BF16)) + b_ref[pl.ds(pl.program_id(0) // col_tiles, 1), :]

    @pl.when(pl.program_id(0) < rope_steps)
    def _():
        ang = pos_ref[...].astype(F32) * inv_freq_ref[...]
        cos_ref[...] = jnp.cos(ang)
        sin_ref[...] = jnp.sin(ang) * sign_ref[...]


def _adaln_rope(c, ada_w, ada_b, positions, tn=1024, tm=1024):
    depth, d, n = ada_w.shape
    bsz = c.shape[0]
    col_tiles = n // tn
    m = positions.size
    rope_steps = m // tm
    assert rope_steps <= depth * col_tiles
    half = DIFF_HEAD_DIM // 2
    inv_freq = ROPE_THETA ** (-jnp.arange(0, DIFF_HEAD_DIM, 2, dtype=F32) / DIFF_HEAD_DIM)
    inv_freq = jnp.concatenate([inv_freq, inv_freq]).reshape(1, DIFF_HEAD_DIM)
    sign = jnp.concatenate([-jnp.ones((half,), F32), jnp.ones((half,), F32)]).reshape(1, DIFF_HEAD_DIM)
    pos = jnp.broadcast_to(positions.reshape(m, 1), (m, DIFF_HEAD_DIM))
    tab = jax.ShapeDtypeStruct((m, DIFF_HEAD_DIM), F32)

    def rope_tile(s):
        return jnp.minimum(s, rope_steps - 1)

    row = pl.BlockSpec((1, DIFF_HEAD_DIM), lambda s: (0, 0))
    tab_spec = pl.BlockSpec((tm, DIFF_HEAD_DIM), lambda s: (rope_tile(s), 0))
    return pl.pallas_call(
        functools.partial(_adaln_rope_kernel, rope_steps=rope_steps, col_tiles=col_tiles),
        grid=(depth * col_tiles,),
        in_specs=[
            pl.BlockSpec((bsz, d), lambda s: (0, 0)),
            pl.BlockSpec((None, d, tn), lambda s: (s // col_tiles, 0, s % col_tiles)),
            pl.BlockSpec((depth, tn), lambda s: (0, s % col_tiles)),
            tab_spec, row, row,
        ],
        out_specs=[pl.BlockSpec((None, bsz, tn), lambda s: (s // col_tiles, 0, s % col_tiles)), tab_spec, tab_spec],
        out_shape=[jax.ShapeDtypeStruct((depth, bsz, n), F32), tab, tab],
        compiler_params=_params("arbitrary"),
        name="adaln_rope",
    )(c, ada_w, ada_b, pos, inv_freq, sign)


def _pw1_glu_kernel(x_ref, sc_ref, sh_ref, wa_ref, wg_ref, ba_ref, bg_ref, o_ref, wa_bf, wg_bf, *, layer):
    @pl.when(pl.program_id(1) == 0)
    def _():
        _cast_rows(wa_ref, wa_bf)
        _cast_rows(wg_ref, wg_bf)

    for r in range(0, x_ref.shape[0], SUB_ROWS):
        rows = slice(r, r + SUB_ROWS)
        h = (x_ref[rows, :] * (1.0 + sc_ref[0]) + sh_ref[0]).astype(BF16)
        a = _dot(h, wa_bf[...]) + ba_ref[layer:layer + 1, :]
        g = _dot(h, wg_bf[...]) + bg_ref[layer:layer + 1, :]
        o_ref[rows, :] = a * jax.nn.sigmoid(g)


def _pw1_glu(x, sc, sh, w, b, layer, seq, tm=1024, tn=512):
    m, d = x.shape
    n_half = w.shape[2] // 2
    gate_off = n_half // tn
    return pl.pallas_call(
        functools.partial(_pw1_glu_kernel, layer=layer),
        grid=(n_half // tn, m // tm),
        in_specs=[
            pl.BlockSpec((tm, d), lambda j, i: (i, 0)),
            _mod_spec(sc, lambda j, i: i * tm // seq),
            _mod_spec(sh, lambda j, i: i * tm // seq),
            pl.BlockSpec((None, d, tn), lambda j, i: (layer, 0, j)),
            pl.BlockSpec((None, d, tn), lambda j, i: (layer, 0, j + gate_off)),
            pl.BlockSpec((b.shape[0], tn), lambda j, i: (0, j)),
            pl.BlockSpec((b.shape[0], tn), lambda j, i: (0, j + gate_off)),
        ],
        out_specs=pl.BlockSpec((tm, tn), lambda j, i: (i, j)),
        out_shape=jax.ShapeDtypeStruct((m, n_half), F32),
        scratch_shapes=[pltpu.VMEM((d, tn), BF16), pltpu.VMEM((d, tn), BF16)],
        compiler_params=_params("arbitrary", "arbitrary"),
        name="pw1_glu",
    )(x, sc[0], sh[0], w, w, b, b)


def _conv_proj_kernel(cur_ref, halo_ref, wdw_ref, bdw_ref, cg_ref, cb_ref, w2_ref, b2_ref, x_ref, gate_ref,
                      lng_ref, lnb_ref, zero_ref, o_ref, w_bf, buf, y, v_sc, z_sc, *, n_cast, n_tiles,
                      tiles_per_seq, row_chunk, w_layer, ln_layer):
    tt, d = cur_ref.shape
    groups = d // LANES
    nw = w_bf.shape[2]
    n_chunks = d // nw
    gpc = groups // n_chunks
    s = pl.program_id(0)

    @pl.when(s < n_cast)
    def _():
        chunk_rows = w2_ref.shape[0]
        r = pl.multiple_of(s * chunk_rows, chunk_rows)
        for n in range(n_chunks):
            w_bf[n, pl.ds(r, chunk_rows), :] = w2_ref[:, n * nw:(n + 1) * nw].astype(BF16)

    @pl.when(s == n_cast)
    def _():
        v_sc[...] = jnp.zeros_like(v_sc)

    @pl.when(s >= n_cast)
    def _():
        t = jnp.minimum(s - n_cast, n_tiles - 1)
        first = (t % tiles_per_seq) == 0
        for c in range(groups):
            lanes = slice(c * LANES, (c + 1) * LANES)
            buf[c, CONV_HALO:, :] = cur_ref[:, lanes]
            buf[c, :CONV_HALO, :] = jnp.where(first, 0.0, halo_ref[:, lanes])

        tap0 = CONV_HALO - (CONV_WIDTH - 1)

        slot = 0
        for n in range(n_chunks):
            z_sc[n] = _dot(v_sc[slot], w_bf[n])
            dep = None
            for c in range(n * gpc, (n + 1) * gpc):
                for r in range(0, tt, row_chunk):
                    acc = jnp.zeros((row_chunk, LANES), F32)
                    for k in range(CONV_WIDTH):
                        acc = acc + buf[c, r + tap0 + k:r + tap0 + k + row_chunk, :] * wdw_ref[c, k:k + 1, :]
                    acc = acc + bdw_ref[c]
                    y[c, r:r + row_chunk, :] = acc
                    bits = pltpu.bitcast(acc, jnp.int32)
                    for q in range(0, row_chunk, 8):
                        dep = bits[q:q + 8, :] if dep is None else dep | bits[q:q + 8, :]
            slot = (dep & zero_ref[...])[0, 0]

        total = jnp.zeros((tt, nw), F32)
        for n in range(n_chunks):
            lanes = slice(n * nw, (n + 1) * nw)
            zn = DEEPNORM_ALPHA * x_ref[:, lanes] + gate_ref[0, :, lanes] * (z_sc[n] + b2_ref[w_layer:w_layer + 1, lanes])
            z_sc[n] = zn
            total = total + zn
        mu = jnp.sum(total, axis=-1, keepdims=True) * (1.0 / d)
        sq = jnp.zeros((tt, nw), F32)
        for n in range(n_chunks):
            zc = z_sc[n] - mu
            sq = sq + zc * zc
        rstd = lax.rsqrt(jnp.sum(sq, axis=-1, keepdims=True) * (1.0 / d) + LN_EPS)
        for n in range(n_chunks):
            lanes = slice(n * nw, (n + 1) * nw)
            o_ref[:, lanes] = ((z_sc[n] - mu) * rstd * lng_ref[ln_layer:ln_layer + 1, lanes]
                               + lnb_ref[ln_layer:ln_layer + 1, lanes])

        total = jnp.zeros((tt, LANES), F32)
        for c in range(groups):
            total = total + y[c]
        mu = jnp.sum(total, axis=-1, keepdims=True) * (1.0 / d)
        sq = jnp.zeros((tt, LANES), F32)
        for c in range(groups):
            yc = y[c] - mu
            sq = sq + yc * yc
        rstd = lax.rsqrt(jnp.sum(sq, axis=-1, keepdims=True) * (1.0 / d) + LN_EPS)
        for c in range(groups):
            lanes = slice(c * LANES, (c + 1) * LANES)
            z = (y[c] - mu) * rstd * cg_ref[:, lanes] + cb_ref[:, lanes]
            v_sc[0, :, lanes] = (z * jax.nn.sigmoid(z)).astype(v_sc.dtype)


def _conv_proj(u, w_dw, b_dw, cln_g, cln_b, w2, b2, x, gate, ln_g, ln_b, w_layer, ln_layer, seq,
               tt=256, row_chunk=64, chunk_rows=256, nw=256):
    m, d = u.shape
    groups = d // LANES
    halo_per_tile = tt // CONV_HALO
    n_tiles = m // tt
    n_cast = d // chunk_rows
    w_g = jnp.pad(w_dw, ((0, CONV_HALO - CONV_WIDTH), (0, 0))).reshape(CONV_HALO, groups, LANES).transpose(1, 0, 2)
    b_g = b_dw.reshape(groups, 1, LANES)

    def conv_tile(s):
        return jnp.clip(s - n_cast, 0, n_tiles - 1)

    def proj_tile(s):
        return jnp.maximum(s - n_cast - 1, 0)

    vec = pl.BlockSpec((1, d), lambda s: (0, 0))
    kern = functools.partial(_conv_proj_kernel, n_cast=n_cast, n_tiles=n_tiles, tiles_per_seq=seq // tt,
                             row_chunk=row_chunk, w_layer=w_layer, ln_layer=ln_layer)
    return pl.pallas_call(
        kern,
        grid=(n_cast + n_tiles + 1,),
        in_specs=[
            pl.BlockSpec((tt, d), lambda s: (conv_tile(s), 0)),
            pl.BlockSpec((CONV_HALO, d), lambda s: (jnp.maximum(conv_tile(s) * halo_per_tile - 1, 0), 0)),
            pl.BlockSpec((groups, CONV_HALO, LANES), lambda s: (0, 0, 0)),
            pl.BlockSpec((groups, 1, LANES), lambda s: (0, 0, 0)),
            vec, vec,
            pl.BlockSpec((None, chunk_rows, d), lambda s: (w_layer, jnp.minimum(s, n_cast - 1), 0)),
            pl.BlockSpec(b2.shape, lambda s: (0, 0)),
            pl.BlockSpec((tt, d), lambda s: (proj_tile(s), 0)),
            _mod_spec(gate, lambda s: proj_tile(s) * tt // seq),
            pl.BlockSpec(ln_g.shape, lambda s: (0, 0)),
            pl.BlockSpec(ln_b.shape, lambda s: (0, 0)),
            pl.BlockSpec((8, LANES), lambda s: (0, 0)),
        ],
        out_specs=pl.BlockSpec((tt, d), lambda s: (proj_tile(s), 0)),
        out_shape=jax.ShapeDtypeStruct((m, d), F32),
        scratch_shapes=[
            pltpu.VMEM((d // nw, d, nw), BF16),
            pltpu.VMEM((groups, CONV_HALO + tt, LANES), F32),
            pltpu.VMEM((groups, tt, LANES), F32),
            pltpu.VMEM((2, tt, d), BF16),
            pltpu.VMEM((d // nw, tt, nw), F32),
        ],
        compiler_params=_params("arbitrary"),
        name="conv_proj",
    )(u, u, w_g, b_g, cln_g.reshape(1, d), cln_b.reshape(1, d), w2, b2, x, gate[0],
      ln_g, ln_b, jnp.zeros((8, LANES), jnp.int32))


def _proj_res_ln_kernel(*refs, has_bias, n_cast, w_layer, ln_layer):
    if has_bias:
        a_ref, w_ref, bias_ref, x_ref, gate_ref, lng_ref, lnb_ref, o_ref, w_bf = refs
    else:
        a_ref, w_ref, x_ref, gate_ref, lng_ref, lnb_ref, o_ref, w_bf = refs
    s = pl.program_id(0)
    chunk_rows = w_ref.shape[0]

    @pl.when(s < n_cast)
    def _():
        r = pl.multiple_of(s * chunk_rows, chunk_rows)
        w_bf[pl.ds(r, chunk_rows), :] = w_ref[...].astype(BF16)

    @pl.when(s >= n_cast)
    def _():
        w = w_bf[...]
        for r in range(0, a_ref.shape[0], SUB_ROWS):
            rows = slice(r, r + SUB_ROWS)
            y = _dot(a_ref[rows, :], w)
            if has_bias:
                y = y + bias_ref[w_layer:w_layer + 1, :]
            z = DEEPNORM_ALPHA * x_ref[rows, :] + gate_ref[0] * y
            o_ref[rows, :] = _layer_norm(z, lng_ref[ln_layer:ln_layer + 1, :], lnb_ref[ln_layer:ln_layer + 1, :])


def _proj_res_ln(a, w, bias, x, gate, ln_g, ln_b, w_layer, ln_layer, seq, tm=512, chunk_rows=256):
    m, k = a.shape
    d = w.shape[2]
    n_cast = k // chunk_rows

    def tile(s):
        return jnp.maximum(s - n_cast, 0)

    in_specs = [
        pl.BlockSpec((tm, k), lambda s: (tile(s), 0)),
        pl.BlockSpec((None, chunk_rows, d), lambda s: (w_layer, jnp.minimum(s, n_cast - 1), 0)),
    ]
    args = [a, w]
    if bias is not None:
        in_specs.append(pl.BlockSpec(bias.shape, lambda s: (0, 0)))
        args.append(bias)
    in_specs += [
        pl.BlockSpec((tm, d), lambda s: (tile(s), 0)),
        _mod_spec(gate, lambda s: tile(s) * tm // seq),
        pl.BlockSpec(ln_g.shape, lambda s: (0, 0)),
        pl.BlockSpec(ln_b.shape, lambda s: (0, 0)),
    ]
    args += [x, gate[0], ln_g, ln_b]
    return pl.pallas_call(
        functools.partial(_proj_res_ln_kernel, has_bias=bias is not None, n_cast=n_cast, w_layer=w_layer,
                          ln_layer=ln_layer),
        grid=(n_cast + m // tm,),
        in_specs=in_specs,
        out_specs=pl.BlockSpec((tm, d), lambda s: (tile(s), 0)),
        out_shape=jax.ShapeDtypeStruct((m, d), F32),
        scratch_shapes=[pltpu.VMEM((k, d), BF16)],
        compiler_params=_params("arbitrary"),
        name="proj_res_ln",
    )(*args)


def _mlp_kernel(x_ref, sc_ref, sh_ref, gate_ref, w1_ref, b1_ref, w2_ref, b2_ref, lng_ref, lnb_ref, o_ref, h_bf, *,
                layer):
    f = pl.program_id(1)
    last = pl.num_programs(1) - 1

    def chunk(h, w1, w2):
        u = _dot(h, w1) + b1_ref[layer:layer + 1, :]
        return _dot(jnp.square(jnp.maximum(u, 0.0)).astype(BF16), w2)

    @pl.when(f == 0)
    def _():
        w1, w2 = w1_ref[...].astype(BF16), w2_ref[...].astype(BF16)
        for r in range(0, x_ref.shape[0], SUB_ROWS):
            rows = slice(r, r + SUB_ROWS)
            h = (x_ref[rows, :] * (1.0 + sc_ref[0]) + sh_ref[0]).astype(BF16)
            h_bf[rows, :] = h
            o_ref[rows, :] = chunk(h, w1, w2)

    @pl.when(jnp.logical_and(f > 0, f < last))
    def _():
        o_ref[...] += chunk(h_bf[...], w1_ref[...].astype(BF16), w2_ref[...].astype(BF16))

    @pl.when(f == last)
    def _():
        w1, w2 = w1_ref[...].astype(BF16), w2_ref[...].astype(BF16)
        for r in range(0, x_ref.shape[0], SUB_ROWS):
            rows = slice(r, r + SUB_ROWS)
            y = o_ref[rows, :] + chunk(h_bf[rows, :], w1, w2) + b2_ref[layer:layer + 1, :]
            z = DEEPNORM_ALPHA * x_ref[rows, :] + gate_ref[0] * y
            o_ref[rows, :] = _layer_norm(z, lng_ref[layer:layer + 1, :], lnb_ref[layer:layer + 1, :])


def _mlp(x, sc, sh, gate, w1, b1, w2, b2, ln_g, ln_b, layer, seq, tm=1024, tf=512):
    m, d = x.shape
    dff = w1.shape[2]
    sc_spec, sh_spec, gate_spec = [_mod_spec(v, lambda i, f: i * tm // seq) for v in (sc, sh, gate)]
    vec_spec = pl.BlockSpec(b2.shape, lambda i, f: (0, 0))
    return pl.pallas_call(
        functools.partial(_mlp_kernel, layer=layer),
        grid=(m // tm, dff // tf),
        in_specs=[
            pl.BlockSpec((tm, d), lambda i, f: (i, 0), pipeline_mode=pl.Buffered(1)),
            sc_spec, sh_spec, gate_spec,
            pl.BlockSpec((None, d, tf), lambda i, f: (layer, 0, f)),
            pl.BlockSpec((b1.shape[0], tf), lambda i, f: (0, f)),
            pl.BlockSpec((None, tf, d), lambda i, f: (layer, f, 0)),
            vec_spec, vec_spec, vec_spec,
        ],
        out_specs=pl.BlockSpec((tm, d), lambda i, f: (i, 0)),
        out_shape=jax.ShapeDtypeStruct((m, d), F32),
        scratch_shapes=[pltpu.VMEM((tm, d), BF16)],
        compiler_params=_params("arbitrary", "arbitrary"),
        name="mlp",
    )(x, sc[0], sh[0], gate[0], w1, b1, w2, b2, ln_g, ln_b)


def _qkv_rope_kernel(x_ref, sc_ref, sh_ref, w_ref, cos_ref, sin_ref, o_ref, w_bf, *, q_tiles, rope_tiles):
    j = pl.program_id(0)

    @pl.when(pl.program_id(1) == 0)
    def _():
        _cast_rows(w_ref, w_bf)

    tn = o_ref.shape[1]

    def step(rope):
        def run():
            scale = jnp.where(j < q_tiles, DIFF_HEAD_DIM ** -0.5 * LOG2_E, 1.0)
            for r in range(0, x_ref.shape[0], SUB_ROWS):
                rows = slice(r, r + SUB_ROWS)
                h = (x_ref[rows, :] * (1.0 + sc_ref[0]) + sh_ref[0]).astype(BF16)
                y = _dot(h, w_bf[...])
                if not rope:
                    o_ref[rows, :] = y.astype(o_ref.dtype)
                    continue
                cos = cos_ref[rows, :] * scale
                sin = sin_ref[rows, :] * scale
                for c in range(tn // DIFF_HEAD_DIM):
                    lanes = slice(c * DIFF_HEAD_DIM, (c + 1) * DIFF_HEAD_DIM)
                    t = y[:, lanes]
                    o_ref[rows, lanes] = (t * cos + pltpu.roll(t, DIFF_HEAD_DIM // 2, 1) * sin).astype(o_ref.dtype)

        return run

    pl.when(j < rope_tiles)(step(True))
    pl.when(j >= rope_tiles)(step(False))


def _qkv_rope(x, sc, sh, w, cos, sin, layer, seq, tm=1024, tn=1024):
    m, d = x.shape
    n = w.shape[2]
    kern = functools.partial(_qkv_rope_kernel, q_tiles=D_MODEL // tn, rope_tiles=2 * D_MODEL // tn)
    return pl.pallas_call(
        kern,
        grid=(n // tn, m // tm),
        in_specs=[
            pl.BlockSpec((tm, d), lambda j, i: (i, 0)),
            _mod_spec(sc, lambda j, i: i * tm // seq),
            _mod_spec(sh, lambda j, i: i * tm // seq),
            pl.BlockSpec((None, d, tn), lambda j, i: (layer, 0, j)),
            pl.BlockSpec((tm, DIFF_HEAD_DIM), lambda j, i: (i, 0)),
            pl.BlockSpec((tm, DIFF_HEAD_DIM), lambda j, i: (i, 0)),
        ],
        out_specs=pl.BlockSpec((tm, tn), lambda j, i: (i, j)),
        out_shape=jax.ShapeDtypeStruct((m, n), BF16),
        scratch_shapes=[pltpu.VMEM((d, tn), BF16)],
        compiler_params=_params("arbitrary", "arbitrary"),
        name="qkv_rope",
    )(x, sc[0], sh[0], w, cos, sin)


def _diff_attn_kernel(q_ref, k_ref, v_ref, lq1_ref, lk1_ref, lq2_ref, lk2_ref, g_ref, o_ref, *, lambda_init, layer):
    tq = q_ref.shape[0]
    dh = DIFF_HEAD_DIM
    vd = DIFF_V_DIM
    layer_row = slice(layer, layer + 1)
    lam = (jnp.exp(jnp.sum(lq1_ref[layer_row, :] * lk1_ref[layer_row, :], axis=-1, keepdims=True))
           - jnp.exp(jnp.sum(lq2_ref[layer_row, :] * lk2_ref[layer_row, :], axis=-1, keepdims=True)) + lambda_init)
    subln_g = g_ref[layer_row, :]

    def head_stages(e, n_chunks, row, col):
        st = {"s": [[None] * n_chunks, [None] * n_chunks], "p": [[None] * n_chunks, [None] * n_chunks],
              "mx": [None, None], "sum": [None, None], "coef": [None, None], "acc": None}

        def scores(c):
            def step():
                for mp in range(2):
                    lanes = slice(e * vd + mp * dh, e * vd + (mp + 1) * dh)
                    s = lax.dot_general(q_ref[:, lanes], k_ref[c * tq:(c + 1) * tq, lanes], (((1,), (1,)), ((), ())),
                                        preferred_element_type=F32)
                    if c == n_chunks - 1:
                        s = jnp.where(row >= col, s, -jnp.inf)
                    st["s"][mp][c] = s
                    part = jnp.maximum(s[:, :LANES], s[:, LANES:])
                    st["mx"][mp] = part if c == 0 else jnp.maximum(st["mx"][mp], part)
            return step

        def row_max():
            for mp in range(2):
                st["mx"][mp] = jnp.max(st["mx"][mp], axis=-1, keepdims=True)

        def exps(c):
            def step():
                for mp in range(2):
                    p = jnp.exp2(st["s"][mp][c] - st["mx"][mp])
                    st["p"][mp][c] = p
                    part = p[:, :LANES] + p[:, LANES:]
                    st["sum"][mp] = part if c == 0 else st["sum"][mp] + part
            return step

        def coefs():
            l1, l2 = [jnp.sum(st["sum"][mp], axis=-1, keepdims=True) for mp in range(2)]
            st["coef"] = [1.0 / l1, lam * l1 / l2]

        def values(c):
            def step():
                a = st["p"][0][c] - st["p"][1][c] * st["coef"][1]
                o = _dot(a.astype(v_ref.dtype), v_ref[c * tq:(c + 1) * tq, e * vd:(e + 1) * vd])
                st["acc"] = o if c == 0 else st["acc"] + o
            return step

        def finish():
            o = st["acc"] * st["coef"][0]
            o = o * lax.rsqrt(jnp.mean(o * o, axis=-1, keepdims=True) + RMS_EPS)
            o = o * subln_g * (1.0 - lambda_init)
            o_ref[:, e * vd:(e + 1) * vd] = o.astype(o_ref.dtype)

        chunks = range(n_chunks)
        return ([scores(c) for c in chunks],
                [row_max] + [exps(c) for c in chunks] + [coefs],
                [values(c) for c in chunks] + [finish])

    def tile(n_chunks):
        def run():
            row = lax.broadcasted_iota(jnp.int32, (tq, tq), 0)
            col = lax.broadcasted_iota(jnp.int32, (tq, tq), 1)
            heads = [head_stages(e, n_chunks, row, col) for e in range(q_ref.shape[1] // vd)]
            for t in range(len(heads) + 2):
                live = [heads[t - k][k] for k in range(3) if 0 <= t - k < len(heads)]
                for j in range(max(len(steps) for steps in live)):
                    for steps in live:
                        if j < len(steps):
                            steps[j]()

        return run

    i = pl.program_id(2)
    for n_chunks in range(1, k_ref.shape[0] // tq + 1):
        pl.when(i == n_chunks - 1)(tile(n_chunks))


def _diff_attn(qkv, lq1, lk1, lq2, lk2, subln_g, layer, bsz, seq, lambda_init, tq=256, heads_per_step=4):
    m = qkv.shape[0]
    nq = seq // tq
    vd = DIFF_V_DIM
    gw = heads_per_step * vd
    groups = DIFF_HEADS // heads_per_step
    lam_spec = pl.BlockSpec(lq1.shape, lambda b, h, i: (0, 0))
    return pl.pallas_call(
        functools.partial(_diff_attn_kernel, lambda_init=lambda_init, layer=layer),
        grid=(bsz, groups, nq),
        in_specs=[
            pl.BlockSpec((tq, gw), lambda b, h, i: (b * nq + i, h)),
            pl.BlockSpec((seq, gw), lambda b, h, i: (b, groups + h)),
            pl.BlockSpec((seq, gw), lambda b, h, i: (b, 2 * groups + h)),
            lam_spec, lam_spec, lam_spec, lam_spec,
            pl.BlockSpec(subln_g.shape, lambda b, h, i: (0, 0)),
        ],
        out_specs=pl.BlockSpec((tq, gw), lambda b, h, i: (b * nq + i, h)),
        out_shape=jax.ShapeDtypeStruct((m, DIFF_HEADS * vd), BF16),
        compiler_params=_params("arbitrary", "arbitrary", "arbitrary"),
        name="diff_attn",
    )(qkv, qkv, qkv, lq1, lk1, lq2, lk2, subln_g)


def kernel(x, c, positions, ada_w, ada_b, ln_mix_g, ln_mix_b, ln_ffn_g, ln_ffn_b, conv_pw1_w, conv_pw1_b, conv_dw_w, conv_dw_b, conv_ln_g, conv_ln_b, conv_pw2_w, conv_pw2_b, attn_qkv_w, attn_lq1, attn_lk1, attn_lq2, attn_lk2, attn_subln_g, attn_o_w, mlp_w1, mlp_b1, mlp_w2, mlp_b2):
    bsz, seq, d = x.shape
    xf = x.reshape(bsz * seq, d)
    mod, cos, sin = _adaln_rope(c, ada_w, ada_b, positions)
    mod = mod.reshape(DEPTH, bsz, 6, 1, d)
    for i in range(DEPTH):
        sh_m, sc_m, g_m, sh_f, sc_f, g_f = [(mod, i, k) for k in range(6)]
        j = i // N_MIXERS
        if i % N_MIXERS == 0:
            u = _pw1_glu(xf, sc_m, sh_m, conv_pw1_w, conv_pw1_b, j, seq)
            xf = _conv_proj(u, conv_dw_w[j], conv_dw_b[j], conv_ln_g[j], conv_ln_b[j], conv_pw2_w, conv_pw2_b,
                            xf, g_m, ln_mix_g, ln_mix_b, j, i, seq)
        else:
            lambda_init = 0.8 - 0.6 * math.exp(-0.3 * i)
            qkv = _qkv_rope(xf, sc_m, sh_m, attn_qkv_w, cos, sin, j, seq)
            o = _diff_attn(qkv, attn_lq1, attn_lk1, attn_lq2, attn_lk2, attn_subln_g, j, bsz, seq, lambda_init)
            xf = _proj_res_ln(o, attn_o_w, None, xf, g_m, ln_mix_g, ln_mix_b, j, i, seq)
        xf = _mlp(xf, sc_f, sh_f, g_f, mlp_w1, mlp_b1, mlp_w2, mlp_b2, ln_ffn_g, ln_ffn_b, i, seq)
    return xf.reshape(bsz, seq, d)
```

```python
import functools
import math

import jax
import jax.numpy as jnp
from jax import lax
from jax.experimental import pallas as pl
from jax.experimental.pallas import tpu as pltpu

F32 = jnp.float32
BF16 = jnp.bfloat16

D_MODEL = 2048
DEPTH = 2
N_MIXERS = 2
CONV_WIDTH = 31
DIFF_HEADS = 8
DIFF_HEAD_DIM = D_MODEL // (2 * DIFF_HEADS)
DIFF_V_DIM = 2 * DIFF_HEAD_DIM
D_FF = 4 * D_MODEL
ROPE_THETA = 10000.0
DEEPNORM_ALPHA = (2.0 * DEPTH) ** 0.25
LN_EPS = 1e-5
RMS_EPS = 1e-5
LOG2_E = math.log2(math.e)

LANES = 128
SUB_ROWS = 256
CONV_HALO = 32
VMEM_LIMIT = 60 * 1024 * 1024


def _dot(a, b):
    return jnp.dot(a, b, preferred_element_type=F32)


def _layer_norm(z, g, b):
    mu = jnp.mean(z, axis=-1, keepdims=True)
    zc = z - mu
    var = jnp.mean(zc * zc, axis=-1, keepdims=True)
    return zc * lax.rsqrt(var + LN_EPS) * g + b


def _cast_rows(src_ref, dst_ref, rows_per_step=256):
    n = src_ref.shape[0] // rows_per_step

    def body(i, carry):
        r = pl.multiple_of(i * rows_per_step, rows_per_step)
        dst_ref[pl.ds(r, rows_per_step), :] = src_ref[pl.ds(r, rows_per_step), :].astype(dst_ref.dtype)
        return carry

    lax.fori_loop(0, n, body, 0)


def _params(*sem):
    return pltpu.CompilerParams(dimension_semantics=sem, vmem_limit_bytes=VMEM_LIMIT)


def _mod_spec(mod, batch_of):
    arr, layer, comp = mod
    return pl.BlockSpec((None, None, 1, 1, arr.shape[-1]), lambda *g: (layer, batch_of(*g), comp, 0, 0))


def _adaln_rope_kernel(c_ref, w_ref, b_ref, pos_ref, inv_freq_ref, sign_ref, o_ref, cos_ref, sin_ref, *,
                       rope_steps, col_tiles):
    c = c_ref[...]
    cond = (c * jax.nn.sigmoid(c)).astype(BF16)
    o_ref[...] = _dot(cond, w_ref[...].astype(BF16)) + b_ref[pl.ds(pl.program_id(0) // col_tiles, 1), :]

    @pl.when(pl.program_id(0) < rope_steps)
    def _():
        ang = pos_ref[...].astype(F32) * inv_freq_ref[...]
        cos_ref[...] = jnp.cos(ang)
        sin_ref[...] = jnp.sin(ang) * sign_ref[...]


def _adaln_rope(c, ada_w, ada_b, positions, tn=1024, tm=1024):
    depth, d, n = ada_w.shape
    bsz = c.shape[0]
    col_tiles = n // tn
    m = positions.size
    rope_steps = m // tm
    assert rope_steps <= depth * col_tiles
    half = DIFF_HEAD_DIM // 2
    inv_freq = ROPE_THETA ** (-jnp.arange(0, DIFF_HEAD_DIM, 2, dtype=F32) / DIFF_HEAD_DIM)
    inv_freq = jnp.concatenate([inv_freq, inv_freq]).reshape(1, DIFF_HEAD_DIM)
    sign = jnp.concatenate([-jnp.ones((half,), F32), jnp.ones((half,), F32)]).reshape(1, DIFF_HEAD_DIM)
    pos = jnp.broadcast_to(positions.reshape(m, 1), (m, DIFF_HEAD_DIM))
    tab = jax.ShapeDtypeStruct((m, DIFF_HEAD_DIM), F32)

    def rope_tile(s):
        return jnp.minimum(s, rope_steps - 1)

    row = pl.BlockSpec((1, DIFF_HEAD_DIM), lambda s: (0, 0))
    tab_spec = pl.BlockSpec((tm, DIFF_HEAD_DIM), lambda s: (rope_tile(s), 0))
    return pl.pallas_call(
        functools.partial(_adaln_rope_kernel, rope_steps=rope_steps, col_tiles=col_tiles),
        grid=(depth * col_tiles,),
        in_specs=[
            pl.BlockSpec((bsz, d), lambda s: (0, 0)),
            pl.BlockSpec((None, d, tn), lambda s: (s // col_tiles, 0, s % col_tiles)),
            pl.BlockSpec((depth, tn), lambda s: (0, s % col_tiles)),
            tab_spec, row, row,
        ],
        out_specs=[pl.BlockSpec((None, bsz, tn), lambda s: (s // col_tiles, 0, s % col_tiles)), tab_spec, tab_spec],
        out_shape=[jax.ShapeDtypeStruct((depth, bsz, n), F32), tab, tab],
        compiler_params=_params("arbitrary"),
        name="adaln_rope",
    )(c, ada_w, ada_b, pos, inv_freq, sign)


def _pw1_glu_kernel(x_ref, sc_ref, sh_ref, wa_ref, wg_ref, ba_ref, bg_ref, o_ref, wa_bf, wg_bf, *, layer):
    @pl.when(pl.program_id(1) == 0)
    def _():
        _cast_rows(wa_ref, wa_bf)
        _cast_rows(wg_ref, wg_bf)

    for r in range(0, x_ref.shape[0], SUB_ROWS):
        rows = slice(r, r + SUB_ROWS)
        h = (x_ref[rows, :] * (1.0 + sc_ref[0]) + sh_ref[0]).astype(BF16)
        a = _dot(h, wa_bf[...]) + ba_ref[layer:layer + 1, :]
        g = _dot(h, wg_bf[...]) + bg_ref[layer:layer + 1, :]
        o_ref[rows, :] = a * jax.nn.sigmoid(g)


def _pw1_glu(x, sc, sh, w, b, layer, seq, tm=1024, tn=512):
    m, d = x.shape
    n_half = w.shape[2] // 2
    gate_off = n_half // tn
    return pl.pallas_call(
        functools.partial(_pw1_glu_kernel, layer=layer),
        grid=(n_half // tn, m // tm),
        in_specs=[
            pl.BlockSpec((tm, d), lambda j, i: (i, 0)),
            _mod_spec(sc, lambda j, i: i * tm // seq),
            _mod_spec(sh, lambda j, i: i * tm // seq),
            pl.BlockSpec((None, d, tn), lambda j, i: (layer, 0, j)),
            pl.BlockSpec((None, d, tn), lambda j, i: (layer, 0, j + gate_off)),
            pl.BlockSpec((b.shape[0], tn), lambda j, i: (0, j)),
            pl.BlockSpec((b.shape[0], tn), lambda j, i: (0, j + gate_off)),
        ],
        out_specs=pl.BlockSpec((tm, tn), lambda j, i: (i, j)),
        out_shape=jax.ShapeDtypeStruct((m, n_half), F32),
        scratch_shapes=[pltpu.VMEM((d, tn), BF16), pltpu.VMEM((d, tn), BF16)],
        compiler_params=_params("arbitrary", "arbitrary"),
        name="pw1_glu",
    )(x, sc[0], sh[0], w, w, b, b)


def _conv_proj_kernel(cur_ref, halo_ref, wdw_ref, bdw_ref, cg_ref, cb_ref, w2_ref, b2_ref, x_ref, gate_ref,
                      lng_ref, lnb_ref, sc_ref, sh_ref, zero_ref, o_ref, h_ref, w_bf, buf, y, v_sc, z_sc, *, n_cast,
                      n_tiles, tiles_per_seq, row_chunk, w_layer, ln_layer):
    tt, d = cur_ref.shape
    groups = d // LANES
    nw = w_bf.shape[2]
    n_chunks = d // nw
    gpc = groups // n_chunks
    s = pl.program_id(0)

    @pl.when(s < n_cast)
    def _():
        chunk_rows = w2_ref.shape[0]
        r = pl.multiple_of(s * chunk_rows, chunk_rows)
        for n in range(n_chunks):
            w_bf[n, pl.ds(r, chunk_rows), :] = w2_ref[:, n * nw:(n + 1) * nw].astype(BF16)

    @pl.when(s == n_cast)
    def _():
        v_sc[...] = jnp.zeros_like(v_sc)

    @pl.when(s >= n_cast)
    def _():
        t = jnp.minimum(s - n_cast, n_tiles - 1)
        first = (t % tiles_per_seq) == 0
        for c in range(groups):
            lanes = slice(c * LANES, (c + 1) * LANES)
            buf[c, CONV_HALO:, :] = cur_ref[:, lanes]
            buf[c, :CONV_HALO, :] = jnp.where(first, 0.0, halo_ref[:, lanes])

        tap0 = CONV_HALO - (CONV_WIDTH - 1)

        slot = 0
        for n in range(n_chunks):
            z_sc[n] = _dot(v_sc[slot], w_bf[n])
            dep = None
            for c in range(n * gpc, (n + 1) * gpc):
                for r in range(0, tt, row_chunk):
                    acc = jnp.zeros((row_chunk, LANES), F32)
                    for k in range(CONV_WIDTH):
                        acc = acc + buf[c, r + tap0 + k:r + tap0 + k + row_chunk, :] * wdw_ref[c, k:k + 1, :]
                    acc = acc + bdw_ref[c]
                    y[c, r:r + row_chunk, :] = acc
                    bits = pltpu.bitcast(acc, jnp.int32)
                    for q in range(0, row_chunk, 8):
                        dep = bits[q:q + 8, :] if dep is None else dep | bits[q:q + 8, :]
            slot = (dep & zero_ref[...])[0, 0]

        total = jnp.zeros((tt, nw), F32)
        for n in range(n_chunks):
            lanes = slice(n * nw, (n + 1) * nw)
            zn = DEEPNORM_ALPHA * x_ref[:, lanes] + gate_ref[0, :, lanes] * (z_sc[n] + b2_ref[w_layer:w_layer + 1, lanes])
            z_sc[n] = zn
            total = total + zn
        mu = jnp.sum(total, axis=-1, keepdims=True) * (1.0 / d)
        sq = jnp.zeros((tt, nw), F32)
        for n in range(n_chunks):
            zc = z_sc[n] - mu
            sq = sq + zc * zc
        rstd = lax.rsqrt(jnp.sum(sq, axis=-1, keepdims=True) * (1.0 / d) + LN_EPS)
        for n in range(n_chunks):
            lanes = slice(n * nw, (n + 1) * nw)
            out = (z_sc[n] - mu) * rstd * lng_ref[ln_layer:ln_layer + 1, lanes] + lnb_ref[ln_layer:ln_layer + 1, lanes]
            o_ref[:, lanes] = out
            h_ref[:, lanes] = (out * (1.0 + sc_ref[0, :, lanes]) + sh_ref[0, :, lanes]).astype(h_ref.dtype)

        total = jnp.zeros((tt, LANES), F32)
        for c in range(groups):
            total = total + y[c]
        mu = jnp.sum(total, axis=-1, keepdims=True) * (1.0 / d)
        sq = jnp.zeros((tt, LANES), F32)
        for c in range(groups):
            yc = y[c] - mu
            sq = sq + yc * yc
        rstd = lax.rsqrt(jnp.sum(sq, axis=-1, keepdims=True) * (1.0 / d) + LN_EPS)
        for c in range(groups):
            lanes = slice(c * LANES, (c + 1) * LANES)
            z = (y[c] - mu) * rstd * cg_ref[:, lanes] + cb_ref[:, lanes]
            v_sc[0, :, lanes] = (z * jax.nn.sigmoid(z)).astype(v_sc.dtype)


def _conv_proj(u, w_dw, b_dw, cln_g, cln_b, w2, b2, x, gate, ln_g, ln_b, sc_next, sh_next, w_layer, ln_layer, seq,
               tt=256, row_chunk=64, chunk_rows=256, nw=256):
    m, d = u.shape
    groups = d // LANES
    halo_per_tile = tt // CONV_HALO
    n_tiles = m // tt
    n_cast = d // chunk_rows
    w_g = jnp.pad(w_dw, ((0, CONV_HALO - CONV_WIDTH), (0, 0))).reshape(CONV_HALO, groups, LANES).transpose(1, 0, 2)
    b_g = b_dw.reshape(groups, 1, LANES)

    def conv_tile(s):
        return jnp.clip(s - n_cast, 0, n_tiles - 1)

    def proj_tile(s):
        return jnp.maximum(s - n_cast - 1, 0)

    vec = pl.BlockSpec((1, d), lambda s: (0, 0))
    kern = functools.partial(_conv_proj_kernel, n_cast=n_cast, n_tiles=n_tiles, tiles_per_seq=seq // tt,
                             row_chunk=row_chunk, w_layer=w_layer, ln_layer=ln_layer)
    return pl.pallas_call(
        kern,
        grid=(n_cast + n_tiles + 1,),
        in_specs=[
            pl.BlockSpec((tt, d), lambda s: (conv_tile(s), 0)),
            pl.BlockSpec((CONV_HALO, d), lambda s: (jnp.maximum(conv_tile(s) * halo_per_tile - 1, 0), 0)),
            pl.BlockSpec((groups, CONV_HALO, LANES), lambda s: (0, 0, 0)),
            pl.BlockSpec((groups, 1, LANES), lambda s: (0, 0, 0)),
            vec, vec,
            pl.BlockSpec((None, chunk_rows, d), lambda s: (w_layer, jnp.minimum(s, n_cast - 1), 0)),
            pl.BlockSpec(b2.shape, lambda s: (0, 0)),
            pl.BlockSpec((tt, d), lambda s: (proj_tile(s), 0)),
            _mod_spec(gate, lambda s: proj_tile(s) * tt // seq),
            pl.BlockSpec(ln_g.shape, lambda s: (0, 0)),
            pl.BlockSpec(ln_b.shape, lambda s: (0, 0)),
            _mod_spec(sc_next, lambda s: proj_tile(s) * tt // seq),
            _mod_spec(sh_next, lambda s: proj_tile(s) * tt // seq),
            pl.BlockSpec((8, LANES), lambda s: (0, 0)),
        ],
        out_specs=[pl.BlockSpec((tt, d), lambda s: (proj_tile(s), 0))] * 2,
        out_shape=[jax.ShapeDtypeStruct((m, d), F32), jax.ShapeDtypeStruct((m, d), BF16)],
        scratch_shapes=[
            pltpu.VMEM((d // nw, d, nw), BF16),
            pltpu.VMEM((groups, CONV_HALO + tt, LANES), F32),
            pltpu.VMEM((groups, tt, LANES), F32),
            pltpu.VMEM((2, tt, d), BF16),
            pltpu.VMEM((d // nw, tt, nw), F32),
        ],
        compiler_params=_params("arbitrary"),
        name="conv_proj",
    )(u, u, w_g, b_g, cln_g.reshape(1, d), cln_b.reshape(1, d), w2, b2, x, gate[0],
      ln_g, ln_b, sc_next[0], sh_next[0], jnp.zeros((8, LANES), jnp.int32))


def _proj_res_ln_kernel(*refs, has_bias, n_cast, w_layer, ln_layer):
    if has_bias:
        a_ref, w_ref, bias_ref, x_ref, gate_ref, lng_ref, lnb_ref, sc_ref, sh_ref, o_ref, h_ref, w_bf = refs
    else:
        a_ref, w_ref, x_ref, gate_ref, lng_ref, lnb_ref, sc_ref, sh_ref, o_ref, h_ref, w_bf = refs
    s = pl.program_id(0)
    chunk_rows = w_ref.shape[0]

    @pl.when(s < n_cast)
    def _():
        r = pl.multiple_of(s * chunk_rows, chunk_rows)
        w_bf[pl.ds(r, chunk_rows), :] = w_ref[...].astype(BF16)

    @pl.when(s >= n_cast)
    def _():
        w = w_bf[...]
        for r in range(0, a_ref.shape[0], SUB_ROWS):
            rows = slice(r, r + SUB_ROWS)
            y = _dot(a_ref[rows, :], w)
            if has_bias:
                y = y + bias_ref[w_layer:w_layer + 1, :]
            z = DEEPNORM_ALPHA * x_ref[rows, :] + gate_ref[0] * y
            out = _layer_norm(z, lng_ref[ln_layer:ln_layer + 1, :], lnb_ref[ln_layer:ln_layer + 1, :])
            o_ref[rows, :] = out
            h_ref[rows, :] = (out * (1.0 + sc_ref[0]) + sh_ref[0]).astype(h_ref.dtype)


def _proj_res_ln(a, w, bias, x, gate, ln_g, ln_b, sc_next, sh_next, w_layer, ln_layer, seq, tm=512, chunk_rows=256):
    m, k = a.shape
    d = w.shape[2]
    n_cast = k // chunk_rows

    def tile(s):
        return jnp.maximum(s - n_cast, 0)

    in_specs = [
        pl.BlockSpec((tm, k), lambda s: (tile(s), 0)),
        pl.BlockSpec((None, chunk_rows, d), lambda s: (w_layer, jnp.minimum(s, n_cast - 1), 0)),
    ]
    args = [a, w]
    if bias is not None:
        in_specs.append(pl.BlockSpec(bias.shape, lambda s: (0, 0)))
        args.append(bias)
    in_specs += [
        pl.BlockSpec((tm, d), lambda s: (tile(s), 0)),
        _mod_spec(gate, lambda s: tile(s) * tm // seq),
        pl.BlockSpec(ln_g.shape, lambda s: (0, 0)),
        pl.BlockSpec(ln_b.shape, lambda s: (0, 0)),
        _mod_spec(sc_next, lambda s: tile(s) * tm // seq),
        _mod_spec(sh_next, lambda s: tile(s) * tm // seq),
    ]
    args += [x, gate[0], ln_g, ln_b, sc_next[0], sh_next[0]]
    return pl.pallas_call(
        functools.partial(_proj_res_ln_kernel, has_bias=bias is not None, n_cast=n_cast, w_layer=w_layer,
                          ln_layer=ln_layer),
        grid=(n_cast + m // tm,),
        in_specs=in_specs,
        out_specs=[pl.BlockSpec((tm, d), lambda s: (tile(s), 0))] * 2,
        out_shape=[jax.ShapeDtypeStruct((m, d), F32), jax.ShapeDtypeStruct((m, d), BF16)],
        scratch_shapes=[pltpu.VMEM((k, d), BF16)],
        compiler_params=_params("arbitrary"),
        name="proj_res_ln",
    )(*args)


def _mlp_kernel(h_ref, x_hbm, gate_ref, w1_ref, b1_ref, w2_ref, b2_ref, lng_ref, lnb_ref, o_ref, x_vm, x_sem, *,
                layer):
    i = pl.program_id(0)
    f = pl.program_id(1)
    last = pl.num_programs(1) - 1
    tm = o_ref.shape[0]
    x_copy = pltpu.make_async_copy(x_hbm.at[pl.ds(pl.multiple_of(i * tm, tm), tm), :], x_vm, x_sem)

    def chunk(h, w1, w2):
        u = _dot(h, w1) + b1_ref[layer:layer + 1, :]
        return _dot(jnp.square(jnp.maximum(u, 0.0)).astype(BF16), w2)

    @pl.when(f == 0)
    def _():
        x_copy.start()
        o_ref[...] = chunk(h_ref[...], w1_ref[...].astype(BF16), w2_ref[...].astype(BF16))

    @pl.when(jnp.logical_and(f > 0, f < last))
    def _():
        o_ref[...] += chunk(h_ref[...], w1_ref[...].astype(BF16), w2_ref[...].astype(BF16))

    @pl.when(f == last)
    def _():
        x_copy.wait()
        w1, w2 = w1_ref[...].astype(BF16), w2_ref[...].astype(BF16)
        for r in range(0, tm, SUB_ROWS):
            rows = slice(r, r + SUB_ROWS)
            y = o_ref[rows, :] + chunk(h_ref[rows, :], w1, w2) + b2_ref[layer:layer + 1, :]
            z = DEEPNORM_ALPHA * x_vm[rows, :] + gate_ref[0] * y
            o_ref[rows, :] = _layer_norm(z, lng_ref[layer:layer + 1, :], lnb_ref[layer:layer + 1, :])


def _mlp(h, x, gate, w1, b1, w2, b2, ln_g, ln_b, layer, seq, tm=1024, tf=512):
    m, d = x.shape
    dff = w1.shape[2]
    assert dff // tf >= 2
    vec_spec = pl.BlockSpec(b2.shape, lambda i, f: (0, 0))
    return pl.pallas_call(
        functools.partial(_mlp_kernel, layer=layer),
        grid=(m // tm, dff // tf),
        in_specs=[
            pl.BlockSpec((tm, d), lambda i, f: (i, 0)),
            pl.BlockSpec(memory_space=pl.ANY),
            _mod_spec(gate, lambda i, f: i * tm // seq),
            pl.BlockSpec((None, d, tf), lambda i, f: (layer, 0, f)),
            pl.BlockSpec((b1.shape[0], tf), lambda i, f: (0, f)),
            pl.BlockSpec((None, tf, d), lambda i, f: (layer, f, 0)),
            vec_spec, vec_spec, vec_spec,
        ],
        out_specs=pl.BlockSpec((tm, d), lambda i, f: (i, 0)),
        out_shape=jax.ShapeDtypeStruct((m, d), F32),
        scratch_shapes=[pltpu.VMEM((tm, d), F32), pltpu.SemaphoreType.DMA(())],
        compiler_params=_params("arbitrary", "arbitrary"),
        name="mlp",
    )(h, x, gate[0], w1, b1, w2, b2, ln_g, ln_b)


def _qkv_rope_kernel(x_ref, sc_ref, sh_ref, w_ref, cos_ref, sin_ref, o_ref, w_bf, *, q_tiles, rope_tiles):
    j = pl.program_id(0)

    @pl.when(pl.program_id(1) == 0)
    def _():
        _cast_rows(w_ref, w_bf)

    tn = o_ref.shape[1]

    def step(rope):
        def run():
            scale = jnp.where(j < q_tiles, DIFF_HEAD_DIM ** -0.5 * LOG2_E, 1.0)
            for r in range(0, x_ref.shape[0], SUB_ROWS):
                rows = slice(r, r + SUB_ROWS)
                h = (x_ref[rows, :] * (1.0 + sc_ref[0]) + sh_ref[0]).astype(BF16)
                y = _dot(h, w_bf[...])
                if not rope:
                    o_ref[rows, :] = y.astype(o_ref.dtype)
                    continue
                cos = cos_ref[rows, :] * scale
                sin = sin_ref[rows, :] * scale
                for c in range(tn // DIFF_HEAD_DIM):
                    lanes = slice(c * DIFF_HEAD_DIM, (c + 1) * DIFF_HEAD_DIM)
                    t = y[:, lanes]
                    o_ref[rows, lanes] = (t * cos + pltpu.roll(t, DIFF_HEAD_DIM // 2, 1) * sin).astype(o_ref.dtype)

        return run

    pl.when(j < rope_tiles)(step(True))
    pl.when(j >= rope_tiles)(step(False))


def _qkv_rope(x, sc, sh, w, cos, sin, layer, seq, tm=1024, tn=1024):
    m, d = x.shape
    n = w.shape[2]
    kern = functools.partial(_qkv_rope_kernel, q_tiles=D_MODEL // tn, rope_tiles=2 * D_MODEL // tn)
    return pl.pallas_call(
        kern,
        grid=(n // tn, m // tm),
        in_specs=[
            pl.BlockSpec((tm, d), lambda j, i: (i, 0)),
            _mod_spec(sc, lambda j, i: i * tm // seq),
            _mod_spec(sh, lambda j, i: i * tm // seq),
            pl.BlockSpec((None, d, tn), lambda j, i: (layer, 0, j)),
            pl.BlockSpec((tm, DIFF_HEAD_DIM), lambda j, i: (i, 0)),
            pl.BlockSpec((tm, DIFF_HEAD_DIM), lambda j, i: (i, 0)),
        ],
        out_specs=pl.BlockSpec((tm, tn), lambda j, i: (i, j)),
        out_shape=jax.ShapeDtypeStruct((m, n), BF16),
        scratch_shapes=[pltpu.VMEM((d, tn), BF16)],
        compiler_params=_params("arbitrary", "arbitrary"),
        name="qkv_rope",
    )(x, sc[0], sh[0], w, cos, sin)


def _diff_attn_kernel(q_ref, k_ref, v_ref, lq1_ref, lk1_ref, lq2_ref, lk2_ref, g_ref, o_ref, *, lambda_init, layer):
    tq = q_ref.shape[0]
    dh = DIFF_HEAD_DIM
    vd = DIFF_V_DIM
    layer_row = slice(layer, layer + 1)
    lam = (jnp.exp(jnp.sum(lq1_ref[layer_row, :] * lk1_ref[layer_row, :], axis=-1, keepdims=True))
           - jnp.exp(jnp.sum(lq2_ref[layer_row, :] * lk2_ref[layer_row, :], axis=-1, keepdims=True)) + lambda_init)
    subln_g = g_ref[layer_row, :]

    def head_stages(e, n_chunks, row, col):
        st = {"s": [[None] * n_chunks, [None] * n_chunks], "p": [[None] * n_chunks, [None] * n_chunks],
              "mx": [None, None], "sum": [None, None], "coef": [None, None], "acc": None}

        def scores(c):
            def step():
                for mp in range(2):
                    lanes = slice(e * vd + mp * dh, e * vd + (mp + 1) * dh)
                    s = lax.dot_general(q_ref[:, lanes], k_ref[c * tq:(c + 1) * tq, lanes], (((1,), (1,)), ((), ())),
                                        preferred_element_type=F32)
                    if c == n_chunks - 1:
                        s = jnp.where(row >= col, s, -jnp.inf)
                    st["s"][mp][c] = s
                    part = jnp.maximum(s[:, :LANES], s[:, LANES:])
                    st["mx"][mp] = part if c == 0 else jnp.maximum(st["mx"][mp], part)
            return step

        def row_max():
            for mp in range(2):
                st["mx"][mp] = jnp.max(st["mx"][mp], axis=-1, keepdims=True)

        def exps(c):
            def step():
                for mp in range(2):
                    p = jnp.exp2(st["s"][mp][c] - st["mx"][mp])
                    st["p"][mp][c] = p
                    part = p[:, :LANES] + p[:, LANES:]
                    st["sum"][mp] = part if c == 0 else st["sum"][mp] + part
            return step

        def coefs():
            l1, l2 = [jnp.sum(st["sum"][mp], axis=-1, keepdims=True) for mp in range(2)]
            st["coef"] = [1.0 / l1, lam * l1 / l2]

        def values(c):
            def step():
                a = st["p"][0][c] - st["p"][1][c] * st["coef"][1]
                o = _dot(a.astype(v_ref.dtype), v_ref[c * tq:(c + 1) * tq, e * vd:(e + 1) * vd])
                st["acc"] = o if c == 0 else st["acc"] + o
            return step

        def finish():
            o = st["acc"] * st["coef"][0]
            o = o * lax.rsqrt(jnp.mean(o * o, axis=-1, keepdims=True) + RMS_EPS)
            o = o * subln_g * (1.0 - lambda_init)
            o_ref[:, e * vd:(e + 1) * vd] = o.astype(o_ref.dtype)

        chunks = range(n_chunks)
        return ([scores(c) for c in chunks],
                [row_max] + [exps(c) for c in chunks] + [coefs],
                [values(c) for c in chunks] + [finish])

    def tile(n_chunks):
        def run():
            row = lax.broadcasted_iota(jnp.int32, (tq, tq), 0)
            col = lax.broadcasted_iota(jnp.int32, (tq, tq), 1)
            heads = [head_stages(e, n_chunks, row, col) for e in range(q_ref.shape[1] // vd)]
            for t in range(len(heads) + 2):
                live = [heads[t - k][k] for k in range(3) if 0 <= t - k < len(heads)]
                for j in range(max(len(steps) for steps in live)):
                    for steps in live:
                        if j < len(steps):
                            steps[j]()

        return run

    i = pl.program_id(2)
    for n_chunks in range(1, k_ref.shape[0] // tq + 1):
        pl.when(i == n_chunks - 1)(tile(n_chunks))


def _diff_attn(qkv, lq1, lk1, lq2, lk2, subln_g, layer, bsz, seq, lambda_init, tq=256, heads_per_step=4):
    m = qkv.shape[0]
    nq = seq // tq
    vd = DIFF_V_DIM
    gw = heads_per_step * vd
    groups = DIFF_HEADS // heads_per_step
    lam_spec = pl.BlockSpec(lq1.shape, lambda b, h, i: (0, 0))
    return pl.pallas_call(
        functools.partial(_diff_attn_kernel, lambda_init=lambda_init, layer=layer),
        grid=(bsz, groups, nq),
        in_specs=[
            pl.BlockSpec((tq, gw), lambda b, h, i: (b * nq + i, h)),
            pl.BlockSpec((seq, gw), lambda b, h, i: (b, groups + h)),
            pl.BlockSpec((seq, gw), lambda b, h, i: (b, 2 * groups + h)),
            lam_spec, lam_spec, lam_spec, lam_spec,
            pl.BlockSpec(subln_g.shape, lambda b, h, i: (0, 0)),
        ],
        out_specs=pl.BlockSpec((tq, gw), lambda b, h, i: (b * nq + i, h)),
        out_shape=jax.ShapeDtypeStruct((m, DIFF_HEADS * vd), BF16),
        compiler_params=_params("arbitrary", "arbitrary", "arbitrary"),
        name="diff_attn",
    )(qkv, qkv, qkv, lq1, lk1, lq2, lk2, subln_g)


def kernel(x, c, positions, ada_w, ada_b, ln_mix_g, ln_mix_b, ln_ffn_g, ln_ffn_b, conv_pw1_w, conv_pw1_b, conv_dw_w, conv_dw_b, conv_ln_g, conv_ln_b, conv_pw2_w, conv_pw2_b, attn_qkv_w, attn_lq1, attn_lk1, attn_lq2, attn_lk2, attn_subln_g, attn_o_w, mlp_w1, mlp_b1, mlp_w2, mlp_b2):
    bsz, seq, d = x.shape
    xf = x.reshape(bsz * seq, d)
    mod, cos, sin = _adaln_rope(c, ada_w, ada_b, positions)
    mod = mod.reshape(DEPTH, bsz, 6, 1, d)
    for i in range(DEPTH):
        sh_m, sc_m, g_m, sh_f, sc_f, g_f = [(mod, i, k) for k in range(6)]
        j = i // N_MIXERS
        if i % N_MIXERS == 0:
            u = _pw1_glu(xf, sc_m, sh_m, conv_pw1_w, conv_pw1_b, j, seq)
            xf, hf = _conv_proj(u, conv_dw_w[j], conv_dw_b[j], conv_ln_g[j], conv_ln_b[j], conv_pw2_w, conv_pw2_b,
                                xf, g_m, ln_mix_g, ln_mix_b, sc_f, sh_f, j, i, seq)
        else:
            lambda_init = 0.8 - 0.6 * math.exp(-0.3 * i)
            qkv = _qkv_rope(xf, sc_m, sh_m, attn_qkv_w, cos, sin, j, seq)
            o = _diff_attn(qkv, attn_lq1, attn_lk1, attn_lq2, attn_lk2, attn_subln_g, j, bsz, seq, lambda_init)
            xf, hf = _proj_res_ln(o, attn_o_w, None, xf, g_m, ln_mix_g, ln_mix_b, sc_f, sh_f, j, i, seq)
        xf = _mlp(hf, xf, g_f, mlp_w1, mlp_b1, mlp_w2, mlp_b2, ln_ffn_g, ln_ffn_b, i, seq)
    return xf.reshape(bsz, seq, d)
```

```python
import functools
import math

import jax
import jax.numpy as jnp
from jax import lax
from jax.experimental import pallas as pl
from jax.experimental.pallas import tpu as pltpu

F32 = jnp.float32
BF16 = jnp.bfloat16

D_MODEL = 2048
DEPTH = 2
N_MIXERS = 2
CONV_WIDTH = 31
DIFF_HEADS = 8
DIFF_HEAD_DIM = D_MODEL // (2 * DIFF_HEADS)
DIFF_V_DIM = 2 * DIFF_HEAD_DIM
D_FF = 4 * D_MODEL
ROPE_THETA = 10000.0
DEEPNORM_ALPHA = (2.0 * DEPTH) ** 0.25
LN_EPS = 1e-5
RMS_EPS = 1e-5
LOG2_E = math.log2(math.e)

LANES = 128
SUB_ROWS = 256
CONV_HALO = 32
VMEM_LIMIT = 60 * 1024 * 1024


def _dot(a, b):
    return jnp.dot(a, b, preferred_element_type=F32)


def _layer_norm(z, g, b):
    mu = jnp.mean(z, axis=-1, keepdims=True)
    zc = z - mu
    var = jnp.mean(zc * zc, axis=-1, keepdims=True)
    return zc * lax.rsqrt(var + LN_EPS) * g + b


def _cast_rows(src_ref, dst_ref, rows_per_step=256):
    n = src_ref.shape[0] // rows_per_step

    def body(i, carry):
        r = pl.multiple_of(i * rows_per_step, rows_per_step)
        dst_ref[pl.ds(r, rows_per_step), :] = src_ref[pl.ds(r, rows_per_step), :].astype(dst_ref.dtype)
        return carry

    lax.fori_loop(0, n, body, 0)


def _params(*sem):
    return pltpu.CompilerParams(dimension_semantics=sem, vmem_limit_bytes=VMEM_LIMIT)


def _mod_spec(mod, batch_of):
    arr, layer, comp = mod
    return pl.BlockSpec((None, None, 1, 1, arr.shape[-1]), lambda *g: (layer, batch_of(*g), comp, 0, 0))


def _adaln_rope_kernel(c_ref, w_ref, b_ref, pos_ref, inv_freq_ref, sign_ref, o_ref, cos_ref, sin_ref, *,
                       rope_steps, col_tiles):
    c = c_ref[...]
    cond = (c * jax.nn.sigmoid(c)).astype(BF16)
    o_ref[...] = _dot(cond, w_ref[...].astype(BF16)) + b_ref[pl.ds(pl.program_id(0) // col_tiles, 1), :]

    @pl.when(pl.program_id(0) < rope_steps)
    def _():
        ang = pos_ref[...].astype(F32) * inv_freq_ref[...]
        cos_ref[...] = jnp.cos(ang)
        sin_ref[...] = jnp.sin(ang) * sign_ref[...]


def _adaln_rope(c, ada_w, ada_b, positions, tn=2048, tm=1024):
    depth, d, n = ada_w.shape
    bsz = c.shape[0]
    col_tiles = n // tn
    m = positions.size
    rope_steps = m // tm
    assert rope_steps <= depth * col_tiles
    half = DIFF_HEAD_DIM // 2
    inv_freq = ROPE_THETA ** (-jnp.arange(0, DIFF_HEAD_DIM, 2, dtype=F32) / DIFF_HEAD_DIM)
    inv_freq = jnp.concatenate([inv_freq, inv_freq]).reshape(1, DIFF_HEAD_DIM)
    sign = jnp.concatenate([-jnp.ones((half,), F32), jnp.ones((half,), F32)]).reshape(1, DIFF_HEAD_DIM)
    pos = jnp.broadcast_to(positions.reshape(m, 1), (m, DIFF_HEAD_DIM))
    tab = jax.ShapeDtypeStruct((m, DIFF_HEAD_DIM), F32)

    def rope_tile(s):
        return jnp.minimum(s, rope_steps - 1)

    row = pl.BlockSpec((1, DIFF_HEAD_DIM), lambda s: (0, 0))
    tab_spec = pl.BlockSpec((tm, DIFF_HEAD_DIM), lambda s: (rope_tile(s), 0))
    return pl.pallas_call(
        functools.partial(_adaln_rope_kernel, rope_steps=rope_steps, col_tiles=col_tiles),
        grid=(depth * col_tiles,),
        in_specs=[
            pl.BlockSpec((bsz, d), lambda s: (0, 0)),
            pl.BlockSpec((None, d, tn), lambda s: (s // col_tiles, 0, s % col_tiles)),
            pl.BlockSpec((depth, tn), lambda s: (0, s % col_tiles)),
            tab_spec, row, row,
        ],
        out_specs=[pl.BlockSpec((None, bsz, tn), lambda s: (s // col_tiles, 0, s % col_tiles)), tab_spec, tab_spec],
        out_shape=[jax.ShapeDtypeStruct((depth, bsz, n), F32), tab, tab],
        compiler_params=_params("arbitrary"),
        name="adaln_rope",
    )(c, ada_w, ada_b, pos, inv_freq, sign)


def _pw1_glu_kernel(x_ref, sc_ref, sh_ref, wa_ref, wg_ref, ba_ref, bg_ref, o_ref, wa_bf, wg_bf, *, layer):
    @pl.when(pl.program_id(1) == 0)
    def _():
        _cast_rows(wa_ref, wa_bf)
        _cast_rows(wg_ref, wg_bf)

    for r in range(0, x_ref.shape[0], SUB_ROWS):
        rows = slice(r, r + SUB_ROWS)
        h = (x_ref[rows, :] * (1.0 + sc_ref[0]) + sh_ref[0]).astype(BF16)
        a = _dot(h, wa_bf[...]) + ba_ref[layer:layer + 1, :]
        g = _dot(h, wg_bf[...]) + bg_ref[layer:layer + 1, :]
        o_ref[rows, :] = a * jax.nn.sigmoid(g)


def _pw1_glu(x, sc, sh, w, b, layer, seq, tm=1024, tn=512):
    m, d = x.shape
    n_half = w.shape[2] // 2
    gate_off = n_half // tn
    return pl.pallas_call(
        functools.partial(_pw1_glu_kernel, layer=layer),
        grid=(n_half // tn, m // tm),
        in_specs=[
            pl.BlockSpec((tm, d), lambda j, i: (i, 0)),
            _mod_spec(sc, lambda j, i: i * tm // seq),
            _mod_spec(sh, lambda j, i: i * tm // seq),
            pl.BlockSpec((None, d, tn), lambda j, i: (layer, 0, j)),
            pl.BlockSpec((None, d, tn), lambda j, i: (layer, 0, j + gate_off)),
            pl.BlockSpec((b.shape[0], tn), lambda j, i: (0, j)),
            pl.BlockSpec((b.shape[0], tn), lambda j, i: (0, j + gate_off)),
        ],
        out_specs=pl.BlockSpec((tm, tn), lambda j, i: (i, j)),
        out_shape=jax.ShapeDtypeStruct((m, n_half), F32),
        scratch_shapes=[pltpu.VMEM((d, tn), BF16), pltpu.VMEM((d, tn), BF16)],
        compiler_params=_params("arbitrary", "arbitrary"),
        name="pw1_glu",
    )(x, sc[0], sh[0], w, w, b, b)


def _conv_proj_kernel(cur_ref, halo_ref, wdw_ref, bdw_ref, cg_ref, cb_ref, w2_ref, b2_ref, x_ref, gate_ref,
                      lng_ref, lnb_ref, sc_ref, sh_ref, zero_ref, o_ref, h_ref, w_bf, buf, y, v_sc, z_sc, *, n_cast,
                      n_tiles, tiles_per_seq, row_chunk, w_layer, ln_layer):
    tt, d = cur_ref.shape
    groups = d // LANES
    nw = w_bf.shape[2]
    n_chunks = d // nw
    gpc = groups // n_chunks
    s = pl.program_id(0)

    @pl.when(s < n_cast)
    def _():
        chunk_rows = w2_ref.shape[0]
        r = pl.multiple_of(s * chunk_rows, chunk_rows)
        for n in range(n_chunks):
            w_bf[n, pl.ds(r, chunk_rows), :] = w2_ref[:, n * nw:(n + 1) * nw].astype(BF16)

    @pl.when(s == n_cast)
    def _():
        v_sc[...] = jnp.zeros_like(v_sc)

    @pl.when(s >= n_cast)
    def _():
        t = jnp.minimum(s - n_cast, n_tiles - 1)
        first = (t % tiles_per_seq) == 0
        for c in range(groups):
            lanes = slice(c * LANES, (c + 1) * LANES)
            buf[c, CONV_HALO:, :] = cur_ref[:, lanes]
            buf[c, :CONV_HALO, :] = jnp.where(first, 0.0, halo_ref[:, lanes])

        tap0 = CONV_HALO - (CONV_WIDTH - 1)

        slot = 0
        for n in range(n_chunks):
            z_sc[n] = _dot(v_sc[slot], w_bf[n])
            dep = None
            for c in range(n * gpc, (n + 1) * gpc):
                for r in range(0, tt, row_chunk):
                    acc = jnp.zeros((row_chunk, LANES), F32)
                    for k in range(CONV_WIDTH):
                        acc = acc + buf[c, r + tap0 + k:r + tap0 + k + row_chunk, :] * wdw_ref[c, k:k + 1, :]
                    acc = acc + bdw_ref[c]
                    y[c, r:r + row_chunk, :] = acc
                    bits = pltpu.bitcast(acc, jnp.int32)
                    for q in range(0, row_chunk, 8):
                        dep = bits[q:q + 8, :] if dep is None else dep | bits[q:q + 8, :]
            slot = (dep & zero_ref[...])[0, 0]

        total = jnp.zeros((tt, nw), F32)
        for n in range(n_chunks):
            lanes = slice(n * nw, (n + 1) * nw)
            zn = DEEPNORM_ALPHA * x_ref[:, lanes] + gate_ref[0, :, lanes] * (z_sc[n] + b2_ref[w_layer:w_layer + 1, lanes])
            z_sc[n] = zn
            total = total + zn
        mu = jnp.sum(total, axis=-1, keepdims=True) * (1.0 / d)
        sq = jnp.zeros((tt, nw), F32)
        for n in range(n_chunks):
            zc = z_sc[n] - mu
            sq = sq + zc * zc
        rstd = lax.rsqrt(jnp.sum(sq, axis=-1, keepdims=True) * (1.0 / d) + LN_EPS)
        for n in range(n_chunks):
            lanes = slice(n * nw, (n + 1) * nw)
            out = (z_sc[n] - mu) * rstd * lng_ref[ln_layer:ln_layer + 1, lanes] + lnb_ref[ln_layer:ln_layer + 1, lanes]
            o_ref[:, lanes] = out
            h_ref[:, lanes] = (out * (1.0 + sc_ref[0, :, lanes]) + sh_ref[0, :, lanes]).astype(h_ref.dtype)

        total = jnp.zeros((tt, LANES), F32)
        for c in range(groups):
            total = total + y[c]
        mu = jnp.sum(total, axis=-1, keepdims=True) * (1.0 / d)
        sq = jnp.zeros((tt, LANES), F32)
        for c in range(groups):
            yc = y[c] - mu
            sq = sq + yc * yc
        rstd = lax.rsqrt(jnp.sum(sq, axis=-1, keepdims=True) * (1.0 / d) + LN_EPS)
        for c in range(groups):
            lanes = slice(c * LANES, (c + 1) * LANES)
            z = (y[c] - mu) * rstd * cg_ref[:, lanes] + cb_ref[:, lanes]
            v_sc[0, :, lanes] = (z * jax.nn.sigmoid(z)).astype(v_sc.dtype)


def _conv_proj(u, w_dw, b_dw, cln_g, cln_b, w2, b2, x, gate, ln_g, ln_b, sc_next, sh_next, w_layer, ln_layer, seq,
               tt=256, row_chunk=32, chunk_rows=256, nw=256):
    m, d = u.shape
    groups = d // LANES
    halo_per_tile = tt // CONV_HALO
    n_tiles = m // tt
    n_cast = d // chunk_rows
    w_g = jnp.pad(w_dw, ((0, CONV_HALO - CONV_WIDTH), (0, 0))).reshape(CONV_HALO, groups, LANES).transpose(1, 0, 2)
    b_g = b_dw.reshape(groups, 1, LANES)

    def conv_tile(s):
        return jnp.clip(s - n_cast, 0, n_tiles - 1)

    def proj_tile(s):
        return jnp.maximum(s - n_cast - 1, 0)

    vec = pl.BlockSpec((1, d), lambda s: (0, 0))
    kern = functools.partial(_conv_proj_kernel, n_cast=n_cast, n_tiles=n_tiles, tiles_per_seq=seq // tt,
                             row_chunk=row_chunk, w_layer=w_layer, ln_layer=ln_layer)
    return pl.pallas_call(
        kern,
        grid=(n_cast + n_tiles + 1,),
        in_specs=[
            pl.BlockSpec((tt, d), lambda s: (conv_tile(s), 0)),
            pl.BlockSpec((CONV_HALO, d), lambda s: (jnp.maximum(conv_tile(s) * halo_per_tile - 1, 0), 0)),
            pl.BlockSpec((groups, CONV_HALO, LANES), lambda s: (0, 0, 0)),
            pl.BlockSpec((groups, 1, LANES), lambda s: (0, 0, 0)),
            vec, vec,
            pl.BlockSpec((None, chunk_rows, d), lambda s: (w_layer, jnp.minimum(s, n_cast - 1), 0)),
            pl.BlockSpec(b2.shape, lambda s: (0, 0)),
            pl.BlockSpec((tt, d), lambda s: (proj_tile(s), 0)),
            _mod_spec(gate, lambda s: proj_tile(s) * tt // seq),
            pl.BlockSpec(ln_g.shape, lambda s: (0, 0)),
            pl.BlockSpec(ln_b.shape, lambda s: (0, 0)),
            _mod_spec(sc_next, lambda s: proj_tile(s) * tt // seq),
            _mod_spec(sh_next, lambda s: proj_tile(s) * tt // seq),
            pl.BlockSpec((8, LANES), lambda s: (0, 0)),
        ],
        out_specs=[pl.BlockSpec((tt, d), lambda s: (proj_tile(s), 0))] * 2,
        out_shape=[jax.ShapeDtypeStruct((m, d), F32), jax.ShapeDtypeStruct((m, d), BF16)],
        scratch_shapes=[
            pltpu.VMEM((d // nw, d, nw), BF16),
            pltpu.VMEM((groups, CONV_HALO + tt, LANES), F32),
            pltpu.VMEM((groups, tt, LANES), F32),
            pltpu.VMEM((2, tt, d), BF16),
            pltpu.VMEM((d // nw, tt, nw), F32),
        ],
        compiler_params=_params("arbitrary"),
        name="conv_proj",
    )(u, u, w_g, b_g, cln_g.reshape(1, d), cln_b.reshape(1, d), w2, b2, x, gate[0],
      ln_g, ln_b, sc_next[0], sh_next[0], jnp.zeros((8, LANES), jnp.int32))


def _proj_res_ln_kernel(*refs, has_bias, n_cast, w_layer, ln_layer):
    if has_bias:
        a_ref, w_ref, bias_ref, x_ref, gate_ref, lng_ref, lnb_ref, sc_ref, sh_ref, o_ref, h_ref, w_bf = refs
    else:
        a_ref, w_ref, x_ref, gate_ref, lng_ref, lnb_ref, sc_ref, sh_ref, o_ref, h_ref, w_bf = refs
    s = pl.program_id(0)
    chunk_rows = w_ref.shape[0]

    @pl.when(s < n_cast)
    def _():
        r = pl.multiple_of(s * chunk_rows, chunk_rows)
        w_bf[pl.ds(r, chunk_rows), :] = w_ref[...].astype(BF16)

    @pl.when(s >= n_cast)
    def _():
        w = w_bf[...]
        for r in range(0, a_ref.shape[0], SUB_ROWS):
            rows = slice(r, r + SUB_ROWS)
            y = _dot(a_ref[rows, :], w)
            if has_bias:
                y = y + bias_ref[w_layer:w_layer + 1, :]
            z = DEEPNORM_ALPHA * x_ref[rows, :] + gate_ref[0] * y
            out = _layer_norm(z, lng_ref[ln_layer:ln_layer + 1, :], lnb_ref[ln_layer:ln_layer + 1, :])
            o_ref[rows, :] = out
            h_ref[rows, :] = (out * (1.0 + sc_ref[0]) + sh_ref[0]).astype(h_ref.dtype)


def _proj_res_ln(a, w, bias, x, gate, ln_g, ln_b, sc_next, sh_next, w_layer, ln_layer, seq, tm=512, chunk_rows=256):
    m, k = a.shape
    d = w.shape[2]
    n_cast = k // chunk_rows

    def tile(s):
        return jnp.maximum(s - n_cast, 0)

    in_specs = [
        pl.BlockSpec((tm, k), lambda s: (tile(s), 0)),
        pl.BlockSpec((None, chunk_rows, d), lambda s: (w_layer, jnp.minimum(s, n_cast - 1), 0)),
    ]
    args = [a, w]
    if bias is not None:
        in_specs.append(pl.BlockSpec(bias.shape, lambda s: (0, 0)))
        args.append(bias)
    in_specs += [
        pl.BlockSpec((tm, d), lambda s: (tile(s), 0)),
        _mod_spec(gate, lambda s: tile(s) * tm // seq),
        pl.BlockSpec(ln_g.shape, lambda s: (0, 0)),
        pl.BlockSpec(ln_b.shape, lambda s: (0, 0)),
        _mod_spec(sc_next, lambda s: tile(s) * tm // seq),
        _mod_spec(sh_next, lambda s: tile(s) * tm // seq),
    ]
    args += [x, gate[0], ln_g, ln_b, sc_next[0], sh_next[0]]
    return pl.pallas_call(
        functools.partial(_proj_res_ln_kernel, has_bias=bias is not None, n_cast=n_cast, w_layer=w_layer,
                          ln_layer=ln_layer),
        grid=(n_cast + m // tm,),
        in_specs=in_specs,
        out_specs=[pl.BlockSpec((tm, d), lambda s: (tile(s), 0))] * 2,
        out_shape=[jax.ShapeDtypeStruct((m, d), F32), jax.ShapeDtypeStruct((m, d), BF16)],
        scratch_shapes=[pltpu.VMEM((k, d), BF16)],
        compiler_params=_params("arbitrary"),
        name="proj_res_ln",
    )(*args)


def _mlp_kernel(h_ref, x_hbm, gate_ref, w1_ref, b1_ref, w2_ref, b2_ref, lng_ref, lnb_ref, o_ref, x_vm, x_sem, *,
                layer):
    i = pl.program_id(0)
    f = pl.program_id(1)
    last = pl.num_programs(1) - 1
    tm = o_ref.shape[0]
    x_copy = pltpu.make_async_copy(x_hbm.at[pl.ds(pl.multiple_of(i * tm, tm), tm), :], x_vm, x_sem)

    def chunk(h, w1, w2):
        u = _dot(h, w1) + b1_ref[layer:layer + 1, :]
        return _dot(jnp.square(jnp.maximum(u, 0.0)).astype(BF16), w2)

    @pl.when(f == 0)
    def _():
        x_copy.start()
        o_ref[...] = chunk(h_ref[...], w1_ref[...].astype(BF16), w2_ref[...].astype(BF16))

    @pl.when(jnp.logical_and(f > 0, f < last))
    def _():
        o_ref[...] += chunk(h_ref[...], w1_ref[...].astype(BF16), w2_ref[...].astype(BF16))

    @pl.when(f == last)
    def _():
        x_copy.wait()
        w1, w2 = w1_ref[...].astype(BF16), w2_ref[...].astype(BF16)
        for r in range(0, tm, SUB_ROWS):
            rows = slice(r, r + SUB_ROWS)
            y = o_ref[rows, :] + chunk(h_ref[rows, :], w1, w2) + b2_ref[layer:layer + 1, :]
            z = DEEPNORM_ALPHA * x_vm[rows, :] + gate_ref[0] * y
            o_ref[rows, :] = _layer_norm(z, lng_ref[layer:layer + 1, :], lnb_ref[layer:layer + 1, :])


def _mlp(h, x, gate, w1, b1, w2, b2, ln_g, ln_b, layer, seq, tm=1024, tf=512):
    m, d = x.shape
    dff = w1.shape[2]
    assert dff // tf >= 2
    vec_spec = pl.BlockSpec(b2.shape, lambda i, f: (0, 0))
    return pl.pallas_call(
        functools.partial(_mlp_kernel, layer=layer),
        grid=(m // tm, dff // tf),
        in_specs=[
            pl.BlockSpec((tm, d), lambda i, f: (i, 0)),
            pl.BlockSpec(memory_space=pl.ANY),
            _mod_spec(gate, lambda i, f: i * tm // seq),
            pl.BlockSpec((None, d, tf), lambda i, f: (layer, 0, f)),
            pl.BlockSpec((b1.shape[0], tf), lambda i, f: (0, f)),
            pl.BlockSpec((None, tf, d), lambda i, f: (layer, f, 0)),
            vec_spec, vec_spec, vec_spec,
        ],
        out_specs=pl.BlockSpec((tm, d), lambda i, f: (i, 0)),
        out_shape=jax.ShapeDtypeStruct((m, d), F32),
        scratch_shapes=[pltpu.VMEM((tm, d), F32), pltpu.SemaphoreType.DMA(())],
        compiler_params=_params("arbitrary", "arbitrary"),
        name="mlp",
    )(h, x, gate[0], w1, b1, w2, b2, ln_g, ln_b)


def _qkv_rope_kernel(x_ref, sc_ref, sh_ref, w_ref, cos_ref, sin_ref, o_ref, w_bf, *, q_tiles, rope_tiles):
    j = pl.program_id(0)

    @pl.when(pl.program_id(1) == 0)
    def _():
        _cast_rows(w_ref, w_bf)

    tn = o_ref.shape[1]

    def step(rope):
        def run():
            scale = jnp.where(j < q_tiles, DIFF_HEAD_DIM ** -0.5 * LOG2_E, 1.0)
            for r in range(0, x_ref.shape[0], SUB_ROWS):
                rows = slice(r, r + SUB_ROWS)
                h = (x_ref[rows, :] * (1.0 + sc_ref[0]) + sh_ref[0]).astype(BF16)
                y = _dot(h, w_bf[...])
                if not rope:
                    o_ref[rows, :] = y.astype(o_ref.dtype)
                    continue
                cos = cos_ref[rows, :] * scale
                sin = sin_ref[rows, :] * scale
                for c in range(tn // DIFF_HEAD_DIM):
                    lanes = slice(c * DIFF_HEAD_DIM, (c + 1) * DIFF_HEAD_DIM)
                    t = y[:, lanes]
                    o_ref[rows, lanes] = (t * cos + pltpu.roll(t, DIFF_HEAD_DIM // 2, 1) * sin).astype(o_ref.dtype)

        return run

    pl.when(j < rope_tiles)(step(True))
    pl.when(j >= rope_tiles)(step(False))


def _qkv_rope(x, sc, sh, w, cos, sin, layer, seq, tm=1024, tn=1024):
    m, d = x.shape
    n = w.shape[2]
    kern = functools.partial(_qkv_rope_kernel, q_tiles=D_MODEL // tn, rope_tiles=2 * D_MODEL // tn)
    return pl.pallas_call(
        kern,
        grid=(n // tn, m // tm),
        in_specs=[
            pl.BlockSpec((tm, d), lambda j, i: (i, 0)),
            _mod_spec(sc, lambda j, i: i * tm // seq),
            _mod_spec(sh, lambda j, i: i * tm // seq),
            pl.BlockSpec((None, d, tn), lambda j, i: (layer, 0, j)),
            pl.BlockSpec((tm, DIFF_HEAD_DIM), lambda j, i: (i, 0)),
            pl.BlockSpec((tm, DIFF_HEAD_DIM), lambda j, i: (i, 0)),
        ],
        out_specs=pl.BlockSpec((tm, tn), lambda j, i: (i, j)),
        out_shape=jax.ShapeDtypeStruct((m, n), BF16),
        scratch_shapes=[pltpu.VMEM((d, tn), BF16)],
        compiler_params=_params("arbitrary", "arbitrary"),
        name="qkv_rope",
    )(x, sc[0], sh[0], w, cos, sin)


def _diff_attn_kernel(q_ref, k_ref, v_ref, lq1_ref, lk1_ref, lq2_ref, lk2_ref, g_ref, o_ref, *, lambda_init, layer):
    tq = q_ref.shape[0]
    dh = DIFF_HEAD_DIM
    vd = DIFF_V_DIM
    layer_row = slice(layer, layer + 1)
    lam = (jnp.exp(jnp.sum(lq1_ref[layer_row, :] * lk1_ref[layer_row, :], axis=-1, keepdims=True))
           - jnp.exp(jnp.sum(lq2_ref[layer_row, :] * lk2_ref[layer_row, :], axis=-1, keepdims=True)) + lambda_init)
    subln_g = g_ref[layer_row, :]

    def head_stages(e, n_chunks, row, col):
        st = {"s": [[None] * n_chunks, [None] * n_chunks], "p": [[None] * n_chunks, [None] * n_chunks],
              "mx": [None, None], "sum": [None, None], "coef": [None, None], "acc": None}

        def scores(c):
            def step():
                for mp in range(2):
                    lanes = slice(e * vd + mp * dh, e * vd + (mp + 1) * dh)
                    s = lax.dot_general(q_ref[:, lanes], k_ref[c * tq:(c + 1) * tq, lanes], (((1,), (1,)), ((), ())),
                                        preferred_element_type=F32)
                    if c == n_chunks - 1:
                        s = jnp.where(row >= col, s, -jnp.inf)
                    st["s"][mp][c] = s
                    part = jnp.maximum(s[:, :LANES], s[:, LANES:])
                    st["mx"][mp] = part if c == 0 else jnp.maximum(st["mx"][mp], part)
            return step

        def row_max():
            for mp in range(2):
                st["mx"][mp] = jnp.max(st["mx"][mp], axis=-1, keepdims=True)

        def exps(c):
            def step():
                for mp in range(2):
                    p = jnp.exp2(st["s"][mp][c] - st["mx"][mp])
                    st["p"][mp][c] = p
                    part = p[:, :LANES] + p[:, LANES:]
                    st["sum"][mp] = part if c == 0 else st["sum"][mp] + part
            return step

        def coefs():
            l1, l2 = [jnp.sum(st["sum"][mp], axis=-1, keepdims=True) for mp in range(2)]
            st["coef"] = [1.0 / l1, lam * l1 / l2]

        def values(c):
            def step():
                a = st["p"][0][c] - st["p"][1][c] * st["coef"][1]
                o = _dot(a.astype(v_ref.dtype), v_ref[c * tq:(c + 1) * tq, e * vd:(e + 1) * vd])
                st["acc"] = o if c == 0 else st["acc"] + o
            return step

        def finish():
            o = st["acc"] * st["coef"][0]
            o = o * lax.rsqrt(jnp.mean(o * o, axis=-1, keepdims=True) + RMS_EPS)
            o = o * subln_g * (1.0 - lambda_init)
            o_ref[:, e * vd:(e + 1) * vd] = o.astype(o_ref.dtype)

        chunks = range(n_chunks)
        return ([scores(c) for c in chunks],
                [row_max] + [exps(c) for c in chunks] + [coefs],
                [values(c) for c in chunks] + [finish])

    def tile(n_chunks):
        def run():
            row = lax.broadcasted_iota(jnp.int32, (tq, tq), 0)
            col = lax.broadcasted_iota(jnp.int32, (tq, tq), 1)
            heads = [head_stages(e, n_chunks, row, col) for e in range(q_ref.shape[1] // vd)]
            for t in range(len(heads) + 2):
                live = [heads[t - k][k] for k in range(3) if 0 <= t - k < len(heads)]
                for j in range(max(len(steps) for steps in live)):
                    for steps in live:
                        if j < len(steps):
                            steps[j]()

        return run

    i = pl.program_id(2)
    for n_chunks in range(1, k_ref.shape[0] // tq + 1):
        pl.when(i == n_chunks - 1)(tile(n_chunks))


def _diff_attn(qkv, lq1, lk1, lq2, lk2, subln_g, layer, bsz, seq, lambda_init, tq=256, heads_per_step=4):
    m = qkv.shape[0]
    nq = seq // tq
    vd = DIFF_V_DIM
    gw = heads_per_step * vd
    groups = DIFF_HEADS // heads_per_step
    lam_spec = pl.BlockSpec(lq1.shape, lambda b, h, i: (0, 0))
    return pl.pallas_call(
        functools.partial(_diff_attn_kernel, lambda_init=lambda_init, layer=layer),
        grid=(bsz, groups, nq),
        in_specs=[
            pl.BlockSpec((tq, gw), lambda b, h, i: (b * nq + i, h)),
            pl.BlockSpec((seq, gw), lambda b, h, i: (b, groups + h)),
            pl.BlockSpec((seq, gw), lambda b, h, i: (b, 2 * groups + h)),
            lam_spec, lam_spec, lam_spec, lam_spec,
            pl.BlockSpec(subln_g.shape, lambda b, h, i: (0, 0)),
        ],
        out_specs=pl.BlockSpec((tq, gw), lambda b, h, i: (b * nq + i, h)),
        out_shape=jax.ShapeDtypeStruct((m, DIFF_HEADS * vd), BF16),
        compiler_params=_params("arbitrary", "arbitrary", "arbitrary"),
        name="diff_attn",
    )(qkv, qkv, qkv, lq1, lk1, lq2, lk2, subln_g)


def kernel(x, c, positions, ada_w, ada_b, ln_mix_g, ln_mix_b, ln_ffn_g, ln_ffn_b, conv_pw1_w, conv_pw1_b, conv_dw_w, conv_dw_b, conv_ln_g, conv_ln_b, conv_pw2_w, conv_pw2_b, attn_qkv_w, attn_lq1, attn_lk1, attn_lq2, attn_lk2, attn_subln_g, attn_o_w, mlp_w1, mlp_b1, mlp_w2, mlp_b2):
    bsz, seq, d = x.shape
    xf = x.reshape(bsz * seq, d)
    mod, cos, sin = _adaln_rope(c, ada_w, ada_b, positions)
    mod = mod.reshape(DEPTH, bsz, 6, 1, d)
    for i in range(DEPTH):
        sh_m, sc_m, g_m, sh_f, sc_f, g_f = [(mod, i, k) for k in range(6)]
        j = i // N_MIXERS
        if i % N_MIXERS == 0:
            u = _pw1_glu(xf, sc_m, sh_m, conv_pw1_w, conv_pw1_b, j, seq)
            xf, hf = _conv_proj(u, conv_dw_w[j], conv_dw_b[j], conv_ln_g[j], conv_ln_b[j], conv_pw2_w, conv_pw2_b,
                                xf, g_m, ln_mix_g, ln_mix_b, sc_f, sh_f, j, i, seq)
        else:
            lambda_init = 0.8 - 0.6 * math.exp(-0.3 * i)
            qkv = _qkv_rope(xf, sc_m, sh_m, attn_qkv_w, cos, sin, j, seq)
            o = _diff_attn(qkv, attn_lq1, attn_lk1, attn_lq2, attn_lk2, attn_subln_g, j, bsz, seq, lambda_init)
            xf, hf = _proj_res_ln(o, attn_o_w, None, xf, g_m, ln_mix_g, ln_mix_b, sc_f, sh_f, j, i, seq)
        xf = _mlp(hf, xf, g_f, mlp_w1, mlp_b1, mlp_w2, mlp_b2, ln_ffn_g, ln_ffn_b, i, seq)
    return xf.reshape(bsz, seq, d)
```

```python
import functools
import math

import jax
import jax.numpy as jnp
from jax import lax
from jax.experimental import pallas as pl
from jax.experimental.pallas import tpu as pltpu

F32 = jnp.float32
BF16 = jnp.bfloat16

D_MODEL = 2048
DEPTH = 2
N_MIXERS = 2
CONV_WIDTH = 31
DIFF_HEADS = 8
DIFF_HEAD_DIM = D_MODEL // (2 * DIFF_HEADS)
DIFF_V_DIM = 2 * DIFF_HEAD_DIM
D_FF = 4 * D_MODEL
ROPE_THETA = 10000.0
DEEPNORM_ALPHA = (2.0 * DEPTH) ** 0.25
LN_EPS = 1e-5
RMS_EPS = 1e-5
LOG2_E = math.log2(math.e)

LANES = 128
SUB_ROWS = 256
CONV_HALO = 32
VMEM_LIMIT = 60 * 1024 * 1024


def _dot(a, b):
    return jnp.dot(a, b, preferred_element_type=F32)


def _layer_norm(z, g, b):
    mu = jnp.mean(z, axis=-1, keepdims=True)
    zc = z - mu
    var = jnp.mean(zc * zc, axis=-1, keepdims=True)
    return zc * lax.rsqrt(var + LN_EPS) * g + b


def _cast_rows(src_ref, dst_ref, rows_per_step=256):
    n = src_ref.shape[0] // rows_per_step

    def body(i, carry):
        r = pl.multiple_of(i * rows_per_step, rows_per_step)
        dst_ref[pl.ds(r, rows_per_step), :] = src_ref[pl.ds(r, rows_per_step), :].astype(dst_ref.dtype)
        return carry

    lax.fori_loop(0, n, body, 0)


def _params(*sem):
    return pltpu.CompilerParams(dimension_semantics=sem, vmem_limit_bytes=VMEM_LIMIT)


RING_SLOTS = 3


def _ring_fetch(x_hbm, ring, sems):
    n_rows = pl.num_programs(1)
    step = pl.program_id(0) * n_rows + pl.program_id(1)
    n_steps = pl.num_programs(0) * n_rows
    tm = ring.shape[1]

    def copy(s):
        row0 = pl.multiple_of((s % n_rows) * tm, tm)
        return pltpu.make_async_copy(x_hbm.at[pl.ds(row0, tm), :], ring.at[s % RING_SLOTS], sems.at[s % RING_SLOTS])

    @pl.when(step == 0)
    def _():
        for s0 in range(RING_SLOTS - 1):
            copy(s0).start()

    @pl.when(step + RING_SLOTS - 1 < n_steps)
    def _():
        copy(step + RING_SLOTS - 1).start()

    copy(step).wait()
    return step % RING_SLOTS


def _mod_spec(mod, batch_of):
    arr, layer, comp = mod
    return pl.BlockSpec((None, None, 1, 1, arr.shape[-1]), lambda *g: (layer, batch_of(*g), comp, 0, 0))


def _adaln_rope_kernel(c_ref, w_ref, b_ref, pos_ref, inv_freq_ref, sign_ref, o_ref, cos_ref, sin_ref, *,
                       rope_steps, col_tiles):
    c = c_ref[...]
    cond = (c * jax.nn.sigmoid(c)).astype(BF16)
    o_ref[...] = _dot(cond, w_ref[...].astype(BF16)) + b_ref[pl.ds(pl.program_id(0) // col_tiles, 1), :]

    @pl.when(pl.program_id(0) < rope_steps)
    def _():
        ang = pos_ref[...].astype(F32) * inv_freq_ref[...]
        cos_ref[...] = jnp.cos(ang)
        sin_ref[...] = jnp.sin(ang) * sign_ref[...]


def _adaln_rope(c, ada_w, ada_b, positions, tn=2048, tm=1024):
    depth, d, n = ada_w.shape
    bsz = c.shape[0]
    col_tiles = n // tn
    m = positions.size
    rope_steps = m // tm
    assert rope_steps <= depth * col_tiles
    half = DIFF_HEAD_DIM // 2
    inv_freq = ROPE_THETA ** (-jnp.arange(0, DIFF_HEAD_DIM, 2, dtype=F32) / DIFF_HEAD_DIM)
    inv_freq = jnp.concatenate([inv_freq, inv_freq]).reshape(1, DIFF_HEAD_DIM)
    sign = jnp.concatenate([-jnp.ones((half,), F32), jnp.ones((half,), F32)]).reshape(1, DIFF_HEAD_DIM)
    pos = jnp.broadcast_to(positions.reshape(m, 1), (m, DIFF_HEAD_DIM))
    tab = jax.ShapeDtypeStruct((m, DIFF_HEAD_DIM), F32)

    def rope_tile(s):
        return jnp.minimum(s, rope_steps - 1)

    row = pl.BlockSpec((1, DIFF_HEAD_DIM), lambda s: (0, 0))
    tab_spec = pl.BlockSpec((tm, DIFF_HEAD_DIM), lambda s: (rope_tile(s), 0))
    return pl.pallas_call(
        functools.partial(_adaln_rope_kernel, rope_steps=rope_steps, col_tiles=col_tiles),
        grid=(depth * col_tiles,),
        in_specs=[
            pl.BlockSpec((bsz, d), lambda s: (0, 0)),
            pl.BlockSpec((None, d, tn), lambda s: (s // col_tiles, 0, s % col_tiles)),
            pl.BlockSpec((depth, tn), lambda s: (0, s % col_tiles)),
            tab_spec, row, row,
        ],
        out_specs=[pl.BlockSpec((None, bsz, tn), lambda s: (s // col_tiles, 0, s % col_tiles)), tab_spec, tab_spec],
        out_shape=[jax.ShapeDtypeStruct((depth, bsz, n), F32), tab, tab],
        compiler_params=_params("arbitrary"),
        name="adaln_rope",
    )(c, ada_w, ada_b, pos, inv_freq, sign)


def _pw1_glu_kernel(x_hbm, sc_ref, sh_ref, wa_ref, wg_ref, ba_ref, bg_ref, o_ref, wa_bf, wg_bf, x_ring, x_sems, *,
                    layer):
    x_ref = x_ring.at[_ring_fetch(x_hbm, x_ring, x_sems)]

    @pl.when(pl.program_id(1) == 0)
    def _():
        _cast_rows(wa_ref, wa_bf)
        _cast_rows(wg_ref, wg_bf)

    for r in range(0, x_ref.shape[0], SUB_ROWS):
        rows = slice(r, r + SUB_ROWS)
        h = (x_ref[rows, :] * (1.0 + sc_ref[0]) + sh_ref[0]).astype(BF16)
        a = _dot(h, wa_bf[...]) + ba_ref[layer:layer + 1, :]
        g = _dot(h, wg_bf[...]) + bg_ref[layer:layer + 1, :]
        o_ref[rows, :] = a * jax.nn.sigmoid(g)


def _pw1_glu(x, sc, sh, w, b, layer, seq, tm=1024, tn=512):
    m, d = x.shape
    n_half = w.shape[2] // 2
    gate_off = n_half // tn
    return pl.pallas_call(
        functools.partial(_pw1_glu_kernel, layer=layer),
        grid=(n_half // tn, m // tm),
        in_specs=[
            pl.BlockSpec(memory_space=pl.ANY),
            _mod_spec(sc, lambda j, i: i * tm // seq),
            _mod_spec(sh, lambda j, i: i * tm // seq),
            pl.BlockSpec((None, d, tn), lambda j, i: (layer, 0, j)),
            pl.BlockSpec((None, d, tn), lambda j, i: (layer, 0, j + gate_off)),
            pl.BlockSpec((b.shape[0], tn), lambda j, i: (0, j)),
            pl.BlockSpec((b.shape[0], tn), lambda j, i: (0, j + gate_off)),
        ],
        out_specs=pl.BlockSpec((tm, tn), lambda j, i: (i, j)),
        out_shape=jax.ShapeDtypeStruct((m, n_half), F32),
        scratch_shapes=[pltpu.VMEM((d, tn), BF16), pltpu.VMEM((d, tn), BF16), pltpu.VMEM((RING_SLOTS, tm, d), F32),
                        pltpu.SemaphoreType.DMA((RING_SLOTS,))],
        compiler_params=_params("arbitrary", "arbitrary"),
        name="pw1_glu",
    )(x, sc[0], sh[0], w, w, b, b)


def _conv_proj_kernel(cur_ref, halo_ref, wdw_ref, bdw_ref, cg_ref, cb_ref, w2_ref, b2_ref, x_ref, gate_ref,
                      lng_ref, lnb_ref, sc_ref, sh_ref, zero_ref, o_ref, h_ref, w_bf, buf, y, v_sc, z_sc, *, n_cast,
                      n_tiles, tiles_per_seq, row_chunk, w_layer, ln_layer):
    tt, d = cur_ref.shape
    groups = d // LANES
    nw = w_bf.shape[2]
    n_chunks = d // nw
    gpc = groups // n_chunks
    s = pl.program_id(0)

    @pl.when(s < n_cast)
    def _():
        chunk_rows = w2_ref.shape[0]
        r = pl.multiple_of(s * chunk_rows, chunk_rows)
        for n in range(n_chunks):
            w_bf[n, pl.ds(r, chunk_rows), :] = w2_ref[:, n * nw:(n + 1) * nw].astype(BF16)

    @pl.when(s == n_cast)
    def _():
        v_sc[...] = jnp.zeros_like(v_sc)

    @pl.when(s >= n_cast)
    def _():
        t = jnp.minimum(s - n_cast, n_tiles - 1)
        first = (t % tiles_per_seq) == 0
        for c in range(groups):
            lanes = slice(c * LANES, (c + 1) * LANES)
            buf[c, CONV_HALO:, :] = cur_ref[:, lanes]
            buf[c, :CONV_HALO, :] = jnp.where(first, 0.0, halo_ref[:, lanes])

        tap0 = CONV_HALO - (CONV_WIDTH - 1)

        slot = 0
        for n in range(n_chunks):
            z_sc[n] = _dot(v_sc[slot], w_bf[n])
            dep = None
            for c in range(n * gpc, (n + 1) * gpc):
                for r in range(0, tt, row_chunk):
                    acc = jnp.zeros((row_chunk, LANES), F32)
                    for k in range(CONV_WIDTH):
                        acc = acc + buf[c, r + tap0 + k:r + tap0 + k + row_chunk, :] * wdw_ref[c, k:k + 1, :]
                    acc = acc + bdw_ref[c]
                    y[c, r:r + row_chunk, :] = acc
                    bits = pltpu.bitcast(acc, jnp.int32)
                    for q in range(0, row_chunk, 8):
                        dep = bits[q:q + 8, :] if dep is None else dep | bits[q:q + 8, :]
            slot = (dep & zero_ref[...])[0, 0]

        total = jnp.zeros((tt, nw), F32)
        for n in range(n_chunks):
            lanes = slice(n * nw, (n + 1) * nw)
            zn = DEEPNORM_ALPHA * x_ref[:, lanes] + gate_ref[0, :, lanes] * (z_sc[n] + b2_ref[w_layer:w_layer + 1, lanes])
            z_sc[n] = zn
            total = total + zn
        mu = jnp.sum(total, axis=-1, keepdims=True) * (1.0 / d)
        sq = jnp.zeros((tt, nw), F32)
        for n in range(n_chunks):
            zc = z_sc[n] - mu
            sq = sq + zc * zc
        rstd = lax.rsqrt(jnp.sum(sq, axis=-1, keepdims=True) * (1.0 / d) + LN_EPS)
        for n in range(n_chunks):
            lanes = slice(n * nw, (n + 1) * nw)
            out = (z_sc[n] - mu) * rstd * lng_ref[ln_layer:ln_layer + 1, lanes] + lnb_ref[ln_layer:ln_layer + 1, lanes]
            o_ref[:, lanes] = out
            h_ref[:, lanes] = (out * (1.0 + sc_ref[0, :, lanes]) + sh_ref[0, :, lanes]).astype(h_ref.dtype)

        total = jnp.zeros((tt, LANES), F32)
        for c in range(groups):
            total = total + y[c]
        mu = jnp.sum(total, axis=-1, keepdims=True) * (1.0 / d)
        sq = jnp.zeros((tt, LANES), F32)
        for c in range(groups):
            yc = y[c] - mu
            sq = sq + yc * yc
        rstd = lax.rsqrt(jnp.sum(sq, axis=-1, keepdims=True) * (1.0 / d) + LN_EPS)
        for c in range(groups):
            lanes = slice(c * LANES, (c + 1) * LANES)
            z = (y[c] - mu) * rstd * cg_ref[:, lanes] + cb_ref[:, lanes]
            v_sc[0, :, lanes] = (z * jax.nn.sigmoid(z)).astype(v_sc.dtype)


def _conv_proj(u, w_dw, b_dw, cln_g, cln_b, w2, b2, x, gate, ln_g, ln_b, sc_next, sh_next, w_layer, ln_layer, seq,
               tt=256, row_chunk=32, chunk_rows=256, nw=256):
    m, d = u.shape
    groups = d // LANES
    halo_per_tile = tt // CONV_HALO
    n_tiles = m // tt
    n_cast = d // chunk_rows
    w_g = jnp.pad(w_dw, ((0, CONV_HALO - CONV_WIDTH), (0, 0))).reshape(CONV_HALO, groups, LANES).transpose(1, 0, 2)
    b_g = b_dw.reshape(groups, 1, LANES)

    def conv_tile(s):
        return jnp.clip(s - n_cast, 0, n_tiles - 1)

    def proj_tile(s):
        return jnp.maximum(s - n_cast - 1, 0)

    vec = pl.BlockSpec((1, d), lambda s: (0, 0))
    kern = functools.partial(_conv_proj_kernel, n_cast=n_cast, n_tiles=n_tiles, tiles_per_seq=seq // tt,
                             row_chunk=row_chunk, w_layer=w_layer, ln_layer=ln_layer)
    return pl.pallas_call(
        kern,
        grid=(n_cast + n_tiles + 1,),
        in_specs=[
            pl.BlockSpec((tt, d), lambda s: (conv_tile(s), 0)),
            pl.BlockSpec((CONV_HALO, d), lambda s: (jnp.maximum(conv_tile(s) * halo_per_tile - 1, 0), 0)),
            pl.BlockSpec((groups, CONV_HALO, LANES), lambda s: (0, 0, 0)),
            pl.BlockSpec((groups, 1, LANES), lambda s: (0, 0, 0)),
            vec, vec,
            pl.BlockSpec((None, chunk_rows, d), lambda s: (w_layer, jnp.minimum(s, n_cast - 1), 0)),
            pl.BlockSpec(b2.shape, lambda s: (0, 0)),
            pl.BlockSpec((tt, d), lambda s: (proj_tile(s), 0)),
            _mod_spec(gate, lambda s: proj_tile(s) * tt // seq),
            pl.BlockSpec(ln_g.shape, lambda s: (0, 0)),
            pl.BlockSpec(ln_b.shape, lambda s: (0, 0)),
            _mod_spec(sc_next, lambda s: proj_tile(s) * tt // seq),
            _mod_spec(sh_next, lambda s: proj_tile(s) * tt // seq),
            pl.BlockSpec((8, LANES), lambda s: (0, 0)),
        ],
        out_specs=[pl.BlockSpec((tt, d), lambda s: (proj_tile(s), 0))] * 2,
        out_shape=[jax.ShapeDtypeStruct((m, d), F32), jax.ShapeDtypeStruct((m, d), BF16)],
        scratch_shapes=[
            pltpu.VMEM((d // nw, d, nw), BF16),
            pltpu.VMEM((groups, CONV_HALO + tt, LANES), F32),
            pltpu.VMEM((groups, tt, LANES), F32),
            pltpu.VMEM((2, tt, d), BF16),
            pltpu.VMEM((d // nw, tt, nw), F32),
        ],
        compiler_params=_params("arbitrary"),
        name="conv_proj",
    )(u, u, w_g, b_g, cln_g.reshape(1, d), cln_b.reshape(1, d), w2, b2, x, gate[0],
      ln_g, ln_b, sc_next[0], sh_next[0], jnp.zeros((8, LANES), jnp.int32))


def _proj_res_ln_kernel(*refs, has_bias, n_cast, w_layer, ln_layer):
    if has_bias:
        a_ref, w_ref, bias_ref, x_ref, gate_ref, lng_ref, lnb_ref, sc_ref, sh_ref, o_ref, h_ref, w_bf = refs
    else:
        a_ref, w_ref, x_ref, gate_ref, lng_ref, lnb_ref, sc_ref, sh_ref, o_ref, h_ref, w_bf = refs
    s = pl.program_id(0)
    chunk_rows = w_ref.shape[0]

    @pl.when(s < n_cast)
    def _():
        r = pl.multiple_of(s * chunk_rows, chunk_rows)
        w_bf[pl.ds(r, chunk_rows), :] = w_ref[...].astype(BF16)

    @pl.when(s >= n_cast)
    def _():
        w = w_bf[...]
        for r in range(0, a_ref.shape[0], SUB_ROWS):
            rows = slice(r, r + SUB_ROWS)
            y = _dot(a_ref[rows, :], w)
            if has_bias:
                y = y + bias_ref[w_layer:w_layer + 1, :]
            z = DEEPNORM_ALPHA * x_ref[rows, :] + gate_ref[0] * y
            out = _layer_norm(z, lng_ref[ln_layer:ln_layer + 1, :], lnb_ref[ln_layer:ln_layer + 1, :])
            o_ref[rows, :] = out
            h_ref[rows, :] = (out * (1.0 + sc_ref[0]) + sh_ref[0]).astype(h_ref.dtype)


def _proj_res_ln(a, w, bias, x, gate, ln_g, ln_b, sc_next, sh_next, w_layer, ln_layer, seq, tm=512, chunk_rows=256):
    m, k = a.shape
    d = w.shape[2]
    n_cast = k // chunk_rows

    def tile(s):
        return jnp.maximum(s - n_cast, 0)

    in_specs = [
        pl.BlockSpec((tm, k), lambda s: (tile(s), 0)),
        pl.BlockSpec((None, chunk_rows, d), lambda s: (w_layer, jnp.minimum(s, n_cast - 1), 0)),
    ]
    args = [a, w]
    if bias is not None:
        in_specs.append(pl.BlockSpec(bias.shape, lambda s: (0, 0)))
        args.append(bias)
    in_specs += [
        pl.BlockSpec((tm, d), lambda s: (tile(s), 0)),
        _mod_spec(gate, lambda s: tile(s) * tm // seq),
        pl.BlockSpec(ln_g.shape, lambda s: (0, 0)),
        pl.BlockSpec(ln_b.shape, lambda s: (0, 0)),
        _mod_spec(sc_next, lambda s: tile(s) * tm // seq),
        _mod_spec(sh_next, lambda s: tile(s) * tm // seq),
    ]
    args += [x, gate[0], ln_g, ln_b, sc_next[0], sh_next[0]]
    return pl.pallas_call(
        functools.partial(_proj_res_ln_kernel, has_bias=bias is not None, n_cast=n_cast, w_layer=w_layer,
                          ln_layer=ln_layer),
        grid=(n_cast + m // tm,),
        in_specs=in_specs,
        out_specs=[pl.BlockSpec((tm, d), lambda s: (tile(s), 0))] * 2,
        out_shape=[jax.ShapeDtypeStruct((m, d), F32), jax.ShapeDtypeStruct((m, d), BF16)],
        scratch_shapes=[pltpu.VMEM((k, d), BF16)],
        compiler_params=_params("arbitrary"),
        name="proj_res_ln",
    )(*args)


def _mlp_kernel(h_ref, x_hbm, gate_ref, w1_ref, b1_ref, w2_ref, b2_ref, lng_ref, lnb_ref, o_ref, x_vm, x_sem, *,
                layer):
    i = pl.program_id(0)
    f = pl.program_id(1)
    last = pl.num_programs(1) - 1
    tm = o_ref.shape[0]
    x_copy = pltpu.make_async_copy(x_hbm.at[pl.ds(pl.multiple_of(i * tm, tm), tm), :], x_vm, x_sem)

    def chunk(h, w1, w2):
        u = _dot(h, w1) + b1_ref[layer:layer + 1, :]
        return _dot(jnp.square(jnp.maximum(u, 0.0)).astype(BF16), w2)

    @pl.when(f == 0)
    def _():
        x_copy.start()
        o_ref[...] = chunk(h_ref[...], w1_ref[...].astype(BF16), w2_ref[...].astype(BF16))

    @pl.when(jnp.logical_and(f > 0, f < last))
    def _():
        o_ref[...] += chunk(h_ref[...], w1_ref[...].astype(BF16), w2_ref[...].astype(BF16))

    @pl.when(f == last)
    def _():
        x_copy.wait()
        w1, w2 = w1_ref[...].astype(BF16), w2_ref[...].astype(BF16)
        for r in range(0, tm, SUB_ROWS):
            rows = slice(r, r + SUB_ROWS)
            y = o_ref[rows, :] + chunk(h_ref[rows, :], w1, w2) + b2_ref[layer:layer + 1, :]
            z = DEEPNORM_ALPHA * x_vm[rows, :] + gate_ref[0] * y
            o_ref[rows, :] = _layer_norm(z, lng_ref[layer:layer + 1, :], lnb_ref[layer:layer + 1, :])


def _mlp(h, x, gate, w1, b1, w2, b2, ln_g, ln_b, layer, seq, tm=1024, tf=512):
    m, d = x.shape
    dff = w1.shape[2]
    assert dff // tf >= 2
    vec_spec = pl.BlockSpec(b2.shape, lambda i, f: (0, 0))
    return pl.pallas_call(
        functools.partial(_mlp_kernel, layer=layer),
        grid=(m // tm, dff // tf),
        in_specs=[
            pl.BlockSpec((tm, d), lambda i, f: (i, 0)),
            pl.BlockSpec(memory_space=pl.ANY),
            _mod_spec(gate, lambda i, f: i * tm // seq),
            pl.BlockSpec((None, d, tf), lambda i, f: (layer, 0, f)),
            pl.BlockSpec((b1.shape[0], tf), lambda i, f: (0, f)),
            pl.BlockSpec((None, tf, d), lambda i, f: (layer, f, 0)),
            vec_spec, vec_spec, vec_spec,
        ],
        out_specs=pl.BlockSpec((tm, d), lambda i, f: (i, 0)),
        out_shape=jax.ShapeDtypeStruct((m, d), F32),
        scratch_shapes=[pltpu.VMEM((tm, d), F32), pltpu.SemaphoreType.DMA(())],
        compiler_params=_params("arbitrary", "arbitrary"),
        name="mlp",
    )(h, x, gate[0], w1, b1, w2, b2, ln_g, ln_b)


def _qkv_rope_kernel(x_hbm, sc_ref, sh_ref, w_ref, cos_ref, sin_ref, o_ref, w_bf, x_ring, x_sems, *, q_tiles,
                     rope_tiles):
    j = pl.program_id(0)
    x_ref = x_ring.at[_ring_fetch(x_hbm, x_ring, x_sems)]

    @pl.when(pl.program_id(1) == 0)
    def _():
        _cast_rows(w_ref, w_bf)

    tn = o_ref.shape[1]

    def step(rope):
        def run():
            scale = jnp.where(j < q_tiles, DIFF_HEAD_DIM ** -0.5 * LOG2_E, 1.0)
            for r in range(0, x_ref.shape[0], SUB_ROWS):
                rows = slice(r, r + SUB_ROWS)
                h = (x_ref[rows, :] * (1.0 + sc_ref[0]) + sh_ref[0]).astype(BF16)
                y = _dot(h, w_bf[...])
                if not rope:
                    o_ref[rows, :] = y.astype(o_ref.dtype)
                    continue
                cos = cos_ref[rows, :] * scale
                sin = sin_ref[rows, :] * scale
                for c in range(tn // DIFF_HEAD_DIM):
                    lanes = slice(c * DIFF_HEAD_DIM, (c + 1) * DIFF_HEAD_DIM)
                    t = y[:, lanes]
                    o_ref[rows, lanes] = (t * cos + pltpu.roll(t, DIFF_HEAD_DIM // 2, 1) * sin).astype(o_ref.dtype)

        return run

    pl.when(j < rope_tiles)(step(True))
    pl.when(j >= rope_tiles)(step(False))


def _qkv_rope(x, sc, sh, w, cos, sin, layer, seq, tm=1024, tn=1024):
    m, d = x.shape
    n = w.shape[2]
    kern = functools.partial(_qkv_rope_kernel, q_tiles=D_MODEL // tn, rope_tiles=2 * D_MODEL // tn)
    return pl.pallas_call(
        kern,
        grid=(n // tn, m // tm),
        in_specs=[
            pl.BlockSpec(memory_space=pl.ANY),
            _mod_spec(sc, lambda j, i: i * tm // seq),
            _mod_spec(sh, lambda j, i: i * tm // seq),
            pl.BlockSpec((None, d, tn), lambda j, i: (layer, 0, j)),
            pl.BlockSpec((tm, DIFF_HEAD_DIM), lambda j, i: (i, 0)),
            pl.BlockSpec((tm, DIFF_HEAD_DIM), lambda j, i: (i, 0)),
        ],
        out_specs=pl.BlockSpec((tm, tn), lambda j, i: (i, j)),
        out_shape=jax.ShapeDtypeStruct((m, n), BF16),
        scratch_shapes=[pltpu.VMEM((d, tn), BF16), pltpu.VMEM((RING_SLOTS, tm, d), F32),
                        pltpu.SemaphoreType.DMA((RING_SLOTS,))],
        compiler_params=_params("arbitrary", "arbitrary"),
        name="qkv_rope",
    )(x, sc[0], sh[0], w, cos, sin)


def _diff_attn_kernel(q_ref, k_ref, v_ref, lq1_ref, lk1_ref, lq2_ref, lk2_ref, g_ref, o_ref, *, lambda_init, layer):
    tq = q_ref.shape[0]
    dh = DIFF_HEAD_DIM
    vd = DIFF_V_DIM
    layer_row = slice(layer, layer + 1)
    lam = (jnp.exp(jnp.sum(lq1_ref[layer_row, :] * lk1_ref[layer_row, :], axis=-1, keepdims=True))
           - jnp.exp(jnp.sum(lq2_ref[layer_row, :] * lk2_ref[layer_row, :], axis=-1, keepdims=True)) + lambda_init)
    subln_g = g_ref[layer_row, :]

    def head_stages(e, n_chunks, row, col):
        st = {"s": [[None] * n_chunks, [None] * n_chunks], "p": [[None] * n_chunks, [None] * n_chunks],
              "mx": [None, None], "sum": [None, None], "coef": [None, None], "acc": None}

        def scores(c):
            def step():
                for mp in range(2):
                    lanes = slice(e * vd + mp * dh, e * vd + (mp + 1) * dh)
                    s = lax.dot_general(q_ref[:, lanes], k_ref[c * tq:(c + 1) * tq, lanes], (((1,), (1,)), ((), ())),
                                        preferred_element_type=F32)
                    if c == n_chunks - 1:
                        s = jnp.where(row >= col, s, -jnp.inf)
                    st["s"][mp][c] = s
                    part = jnp.maximum(s[:, :LANES], s[:, LANES:])
                    st["mx"][mp] = part if c == 0 else jnp.maximum(st["mx"][mp], part)
            return step

        def row_max():
            for mp in range(2):
                st["mx"][mp] = jnp.max(st["mx"][mp], axis=-1, keepdims=True)

        def exps(c):
            def step():
                for mp in range(2):
                    p = jnp.exp2(st["s"][mp][c] - st["mx"][mp])
                    st["p"][mp][c] = p
                    part = p[:, :LANES] + p[:, LANES:]
                    st["sum"][mp] = part if c == 0 else st["sum"][mp] + part
            return step

        def coefs():
            l1, l2 = [jnp.sum(st["sum"][mp], axis=-1, keepdims=True) for mp in range(2)]
            st["coef"] = [1.0 / l1, lam * l1 / l2]

        def values(c):
            def step():
                a = st["p"][0][c] - st["p"][1][c] * st["coef"][1]
                o = _dot(a.astype(v_ref.dtype), v_ref[c * tq:(c + 1) * tq, e * vd:(e + 1) * vd])
                st["acc"] = o if c == 0 else st["acc"] + o
            return step

        def finish():
            o = st["acc"] * st["coef"][0]
            o = o * lax.rsqrt(jnp.mean(o * o, axis=-1, keepdims=True) + RMS_EPS)
            o = o * subln_g * (1.0 - lambda_init)
            o_ref[:, e * vd:(e + 1) * vd] = o.astype(o_ref.dtype)

        chunks = range(n_chunks)
        return ([scores(c) for c in chunks],
                [row_max] + [exps(c) for c in chunks] + [coefs],
                [values(c) for c in chunks] + [finish])

    def tile(n_chunks):
        def run():
            row = lax.broadcasted_iota(jnp.int32, (tq, tq), 0)
            col = lax.broadcasted_iota(jnp.int32, (tq, tq), 1)
            heads = [head_stages(e, n_chunks, row, col) for e in range(q_ref.shape[1] // vd)]
            for t in range(len(heads) + 2):
                live = [heads[t - k][k] for k in range(3) if 0 <= t - k < len(heads)]
                for j in range(max(len(steps) for steps in live)):
                    for steps in live:
                        if j < len(steps):
                            steps[j]()

        return run

    i = pl.program_id(2)
    for n_chunks in range(1, k_ref.shape[0] // tq + 1):
        pl.when(i == n_chunks - 1)(tile(n_chunks))


def _diff_attn(qkv, lq1, lk1, lq2, lk2, subln_g, layer, bsz, seq, lambda_init, tq=256, heads_per_step=4):
    m = qkv.shape[0]
    nq = seq // tq
    vd = DIFF_V_DIM
    gw = heads_per_step * vd
    groups = DIFF_HEADS // heads_per_step
    lam_spec = pl.BlockSpec(lq1.shape, lambda b, h, i: (0, 0))
    return pl.pallas_call(
        functools.partial(_diff_attn_kernel, lambda_init=lambda_init, layer=layer),
        grid=(bsz, groups, nq),
        in_specs=[
            pl.BlockSpec((tq, gw), lambda b, h, i: (b * nq + i, h)),
            pl.BlockSpec((seq, gw), lambda b, h, i: (b, groups + h)),
            pl.BlockSpec((seq, gw), lambda b, h, i: (b, 2 * groups + h)),
            lam_spec, lam_spec, lam_spec, lam_spec,
            pl.BlockSpec(subln_g.shape, lambda b, h, i: (0, 0)),
        ],
        out_specs=pl.BlockSpec((tq, gw), lambda b, h, i: (b * nq + i, h)),
        out_shape=jax.ShapeDtypeStruct((m, DIFF_HEADS * vd), BF16),
        compiler_params=_params("arbitrary", "arbitrary", "arbitrary"),
        name="diff_attn",
    )(qkv, qkv, qkv, lq1, lk1, lq2, lk2, subln_g)


def kernel(x, c, positions, ada_w, ada_b, ln_mix_g, ln_mix_b, ln_ffn_g, ln_ffn_b, conv_pw1_w, conv_pw1_b, conv_dw_w, conv_dw_b, conv_ln_g, conv_ln_b, conv_pw2_w, conv_pw2_b, attn_qkv_w, attn_lq1, attn_lk1, attn_lq2, attn_lk2, attn_subln_g, attn_o_w, mlp_w1, mlp_b1, mlp_w2, mlp_b2):
    bsz, seq, d = x.shape
    xf = x.reshape(bsz * seq, d)
    mod, cos, sin = _adaln_rope(c, ada_w, ada_b, positions)
    mod = mod.reshape(DEPTH, bsz, 6, 1, d)
    for i in range(DEPTH):
        sh_m, sc_m, g_m, sh_f, sc_f, g_f = [(mod, i, k) for k in range(6)]
        j = i // N_MIXERS
        if i % N_MIXERS == 0:
            u = _pw1_glu(xf, sc_m, sh_m, conv_pw1_w, conv_pw1_b, j, seq)
            xf, hf = _conv_proj(u, conv_dw_w[j], conv_dw_b[j], conv_ln_g[j], conv_ln_b[j], conv_pw2_w, conv_pw2_b,
                                xf, g_m, ln_mix_g, ln_mix_b, sc_f, sh_f, j, i, seq)
        else:
            lambda_init = 0.8 - 0.6 * math.exp(-0.3 * i)
            qkv = _qkv_rope(xf, sc_m, sh_m, attn_qkv_w, cos, sin, j, seq)
            o = _diff_attn(qkv, attn_lq1, attn_lk1, attn_lq2, attn_lk2, attn_subln_g, j, bsz, seq, lambda_init)
            xf, hf = _proj_res_ln(o, attn_o_w, None, xf, g_m, ln_mix_g, ln_mix_b, sc_f, sh_f, j, i, seq)
        xf = _mlp(hf, xf, g_f, mlp_w1, mlp_b1, mlp_w2, mlp_b2, ln_ffn_g, ln_ffn_b, i, seq)
    return xf.reshape(bsz, seq, d)
```

```python
import functools
import math

import jax
import jax.numpy as jnp
from jax import lax
from jax.experimental import pallas as pl
from jax.experimental.pallas import tpu as pltpu

F32 = jnp.float32
BF16 = jnp.bfloat16

D_MODEL = 2048
DEPTH = 2
N_MIXERS = 2
CONV_WIDTH = 31
DIFF_HEADS = 8
DIFF_HEAD_DIM = D_MODEL // (2 * DIFF_HEADS)
DIFF_V_DIM = 2 * DIFF_HEAD_DIM
D_FF = 4 * D_MODEL
ROPE_THETA = 10000.0
DEEPNORM_ALPHA = (2.0 * DEPTH) ** 0.25
LN_EPS = 1e-5
RMS_EPS = 1e-5
LOG2_E = math.log2(math.e)

LANES = 128
SUB_ROWS = 256
CONV_HALO = 32
VMEM_LIMIT = 60 * 1024 * 1024


def _dot(a, b):
    return jnp.dot(a, b, preferred_element_type=F32)


def _layer_norm(z, g, b):
    mu = jnp.mean(z, axis=-1, keepdims=True)
    zc = z - mu
    var = jnp.mean(zc * zc, axis=-1, keepdims=True)
    return zc * lax.rsqrt(var + LN_EPS) * g + b


def _cast_rows(src_ref, dst_ref, rows_per_step=256):
    n = src_ref.shape[0] // rows_per_step

    def body(i, carry):
        r = pl.multiple_of(i * rows_per_step, rows_per_step)
        dst_ref[pl.ds(r, rows_per_step), :] = src_ref[pl.ds(r, rows_per_step), :].astype(dst_ref.dtype)
        return carry

    lax.fori_loop(0, n, body, 0)


def _params(*sem):
    return pltpu.CompilerParams(dimension_semantics=sem, vmem_limit_bytes=VMEM_LIMIT)


RING_SLOTS = 3


def _ring_fetch(x_hbm, ring, sems):
    n_rows = pl.num_programs(1)
    step = pl.program_id(0) * n_rows + pl.program_id(1)
    n_steps = pl.num_programs(0) * n_rows
    tm = ring.shape[1]

    def copy(s):
        row0 = pl.multiple_of((s % n_rows) * tm, tm)
        return pltpu.make_async_copy(x_hbm.at[pl.ds(row0, tm), :], ring.at[s % RING_SLOTS], sems.at[s % RING_SLOTS])

    @pl.when(step == 0)
    def _():
        for s0 in range(RING_SLOTS - 1):
            copy(s0).start()

    @pl.when(step + RING_SLOTS - 1 < n_steps)
    def _():
        copy(step + RING_SLOTS - 1).start()

    copy(step).wait()
    return step % RING_SLOTS


def _mod_spec(mod, batch_of):
    arr, layer, comp = mod
    return pl.BlockSpec((None, None, 1, 1, arr.shape[-1]), lambda *g: (layer, batch_of(*g), comp, 0, 0))


def _adaln_rope_kernel(c_ref, w_hbm, b_ref, pos_ref, inv_freq_ref, sign_ref, o_ref, cos_ref, sin_ref, w_ring, w_sems,
                       *, rope_steps, col_tiles):
    s = pl.program_id(0)
    tn = w_ring.shape[2]

    def w_copy(t):
        col0 = pl.multiple_of((t % col_tiles) * tn, tn)
        return pltpu.make_async_copy(w_hbm.at[t // col_tiles, :, pl.ds(col0, tn)], w_ring.at[t % RING_SLOTS],
                                     w_sems.at[t % RING_SLOTS])

    @pl.when(s == 0)
    def _():
        for t0 in range(RING_SLOTS - 1):
            w_copy(t0).start()

    @pl.when(s + RING_SLOTS - 1 < pl.num_programs(0))
    def _():
        w_copy(s + RING_SLOTS - 1).start()

    w_copy(s).wait()
    c = c_ref[...]
    cond = (c * jax.nn.sigmoid(c)).astype(BF16)
    o_ref[...] = _dot(cond, w_ring[s % RING_SLOTS].astype(BF16)) + b_ref[pl.ds(s // col_tiles, 1), :]

    @pl.when(pl.program_id(0) < rope_steps)
    def _():
        ang = pos_ref[...].astype(F32) * inv_freq_ref[...]
        cos_ref[...] = jnp.cos(ang)
        sin_ref[...] = jnp.sin(ang) * sign_ref[...]


def _adaln_rope(c, ada_w, ada_b, positions, tn=1024, tm=1024):
    depth, d, n = ada_w.shape
    bsz = c.shape[0]
    col_tiles = n // tn
    m = positions.size
    rope_steps = m // tm
    assert rope_steps <= depth * col_tiles
    half = DIFF_HEAD_DIM // 2
    inv_freq = ROPE_THETA ** (-jnp.arange(0, DIFF_HEAD_DIM, 2, dtype=F32) / DIFF_HEAD_DIM)
    inv_freq = jnp.concatenate([inv_freq, inv_freq]).reshape(1, DIFF_HEAD_DIM)
    sign = jnp.concatenate([-jnp.ones((half,), F32), jnp.ones((half,), F32)]).reshape(1, DIFF_HEAD_DIM)
    pos = jnp.broadcast_to(positions.reshape(m, 1), (m, DIFF_HEAD_DIM))
    tab = jax.ShapeDtypeStruct((m, DIFF_HEAD_DIM), F32)

    def rope_tile(s):
        return jnp.minimum(s, rope_steps - 1)

    row = pl.BlockSpec((1, DIFF_HEAD_DIM), lambda s: (0, 0))
    tab_spec = pl.BlockSpec((tm, DIFF_HEAD_DIM), lambda s: (rope_tile(s), 0))
    return pl.pallas_call(
        functools.partial(_adaln_rope_kernel, rope_steps=rope_steps, col_tiles=col_tiles),
        grid=(depth * col_tiles,),
        in_specs=[
            pl.BlockSpec((bsz, d), lambda s: (0, 0)),
            pl.BlockSpec(memory_space=pl.ANY),
            pl.BlockSpec((depth, tn), lambda s: (0, s % col_tiles)),
            tab_spec, row, row,
        ],
        out_specs=[pl.BlockSpec((None, bsz, tn), lambda s: (s // col_tiles, 0, s % col_tiles)), tab_spec, tab_spec],
        out_shape=[jax.ShapeDtypeStruct((depth, bsz, n), F32), tab, tab],
        scratch_shapes=[pltpu.VMEM((RING_SLOTS, d, tn), F32), pltpu.SemaphoreType.DMA((RING_SLOTS,))],
        compiler_params=_params("arbitrary"),
        name="adaln_rope",
    )(c, ada_w, ada_b, pos, inv_freq, sign)


def _pw1_glu_kernel(x_hbm, sc_ref, sh_ref, wa_ref, wg_ref, ba_ref, bg_ref, o_ref, wa_bf, wg_bf, x_ring, x_sems, *,
                    layer):
    x_ref = x_ring.at[_ring_fetch(x_hbm, x_ring, x_sems)]

    @pl.when(pl.program_id(1) == 0)
    def _():
        _cast_rows(wa_ref, wa_bf)
        _cast_rows(wg_ref, wg_bf)

    for r in range(0, x_ref.shape[0], SUB_ROWS):
        rows = slice(r, r + SUB_ROWS)
        h = (x_ref[rows, :] * (1.0 + sc_ref[0]) + sh_ref[0]).astype(BF16)
        a = _dot(h, wa_bf[...]) + ba_ref[layer:layer + 1, :]
        g = _dot(h, wg_bf[...]) + bg_ref[layer:layer + 1, :]
        o_ref[rows, :] = a * jax.nn.sigmoid(g)


def _pw1_glu(x, sc, sh, w, b, layer, seq, tm=1024, tn=512):
    m, d = x.shape
    n_half = w.shape[2] // 2
    gate_off = n_half // tn
    return pl.pallas_call(
        functools.partial(_pw1_glu_kernel, layer=layer),
        grid=(n_half // tn, m // tm),
        in_specs=[
            pl.BlockSpec(memory_space=pl.ANY),
            _mod_spec(sc, lambda j, i: i * tm // seq),
            _mod_spec(sh, lambda j, i: i * tm // seq),
            pl.BlockSpec((None, d, tn), lambda j, i: (layer, 0, j)),
            pl.BlockSpec((None, d, tn), lambda j, i: (layer, 0, j + gate_off)),
            pl.BlockSpec((b.shape[0], tn), lambda j, i: (0, j)),
            pl.BlockSpec((b.shape[0], tn), lambda j, i: (0, j + gate_off)),
        ],
        out_specs=pl.BlockSpec((tm, tn), lambda j, i: (i, j)),
        out_shape=jax.ShapeDtypeStruct((m, n_half), F32),
        scratch_shapes=[pltpu.VMEM((d, tn), BF16), pltpu.VMEM((d, tn), BF16), pltpu.VMEM((RING_SLOTS, tm, d), F32),
                        pltpu.SemaphoreType.DMA((RING_SLOTS,))],
        compiler_params=_params("arbitrary", "arbitrary"),
        name="pw1_glu",
    )(x, sc[0], sh[0], w, w, b, b)


def _conv_proj_kernel(cur_ref, halo_ref, wdw_ref, bdw_ref, cg_ref, cb_ref, w2_ref, b2_ref, x_ref, gate_ref,
                      lng_ref, lnb_ref, sc_ref, sh_ref, zero_ref, o_ref, h_ref, w_bf, buf, y, v_sc, z_sc, *, n_cast,
                      n_tiles, tiles_per_seq, row_chunk, w_layer, ln_layer):
    tt, d = cur_ref.shape
    groups = d // LANES
    nw = w_bf.shape[2]
    n_chunks = d // nw
    gpc = groups // n_chunks
    s = pl.program_id(0)

    @pl.when(s < n_cast)
    def _():
        chunk_rows = w2_ref.shape[0]
        r = pl.multiple_of(s * chunk_rows, chunk_rows)
        for n in range(n_chunks):
            w_bf[n, pl.ds(r, chunk_rows), :] = w2_ref[:, n * nw:(n + 1) * nw].astype(BF16)

    @pl.when(s == n_cast)
    def _():
        v_sc[...] = jnp.zeros_like(v_sc)

    @pl.when(s >= n_cast)
    def _():
        t = jnp.minimum(s - n_cast, n_tiles - 1)
        first = (t % tiles_per_seq) == 0
        for c in range(groups):
            lanes = slice(c * LANES, (c + 1) * LANES)
            buf[c, CONV_HALO:, :] = cur_ref[:, lanes]
            buf[c, :CONV_HALO, :] = jnp.where(first, 0.0, halo_ref[:, lanes])

        tap0 = CONV_HALO - (CONV_WIDTH - 1)

        slot = 0
        for n in range(n_chunks):
            z_sc[n] = _dot(v_sc[slot], w_bf[n])
            dep = None
            for c in range(n * gpc, (n + 1) * gpc):
                for r in range(0, tt, row_chunk):
                    acc = jnp.zeros((row_chunk, LANES), F32)
                    for k in range(CONV_WIDTH):
                        acc = acc + buf[c, r + tap0 + k:r + tap0 + k + row_chunk, :] * wdw_ref[c, k:k + 1, :]
                    acc = acc + bdw_ref[c]
                    y[c, r:r + row_chunk, :] = acc
                    bits = pltpu.bitcast(acc, jnp.int32)
                    for q in range(0, row_chunk, 8):
                        dep = bits[q:q + 8, :] if dep is None else dep | bits[q:q + 8, :]
            slot = (dep & zero_ref[...])[0, 0]

        total = jnp.zeros((tt, nw), F32)
        for n in range(n_chunks):
            lanes = slice(n * nw, (n + 1) * nw)
            zn = DEEPNORM_ALPHA * x_ref[:, lanes] + gate_ref[0, :, lanes] * (z_sc[n] + b2_ref[w_layer:w_layer + 1, lanes])
            z_sc[n] = zn
            total = total + zn
        mu = jnp.sum(total, axis=-1, keepdims=True) * (1.0 / d)
        sq = jnp.zeros((tt, nw), F32)
        for n in range(n_chunks):
            zc = z_sc[n] - mu
            sq = sq + zc * zc
        rstd = lax.rsqrt(jnp.sum(sq, axis=-1, keepdims=True) * (1.0 / d) + LN_EPS)
        for n in range(n_chunks):
            lanes = slice(n * nw, (n + 1) * nw)
            out = (z_sc[n] - mu) * rstd * lng_ref[ln_layer:ln_layer + 1, lanes] + lnb_ref[ln_layer:ln_layer + 1, lanes]
            o_ref[:, lanes] = out
            h_ref[:, lanes] = (out * (1.0 + sc_ref[0, :, lanes]) + sh_ref[0, :, lanes]).astype(h_ref.dtype)

        total = jnp.zeros((tt, LANES), F32)
        for c in range(groups):
            total = total + y[c]
        mu = jnp.sum(total, axis=-1, keepdims=True) * (1.0 / d)
        sq = jnp.zeros((tt, LANES), F32)
        for c in range(groups):
            yc = y[c] - mu
            sq = sq + yc * yc
        rstd = lax.rsqrt(jnp.sum(sq, axis=-1, keepdims=True) * (1.0 / d) + LN_EPS)
        for c in range(groups):
            lanes = slice(c * LANES, (c + 1) * LANES)
            z = (y[c] - mu) * rstd * cg_ref[:, lanes] + cb_ref[:, lanes]
            v_sc[0, :, lanes] = (z * jax.nn.sigmoid(z)).astype(v_sc.dtype)


def _conv_proj(u, w_dw, b_dw, cln_g, cln_b, w2, b2, x, gate, ln_g, ln_b, sc_next, sh_next, w_layer, ln_layer, seq,
               tt=256, row_chunk=32, chunk_rows=256, nw=256):
    m, d = u.shape
    groups = d // LANES
    halo_per_tile = tt // CONV_HALO
    n_tiles = m // tt
    n_cast = d // chunk_rows
    w_g = jnp.pad(w_dw, ((0, CONV_HALO - CONV_WIDTH), (0, 0))).reshape(CONV_HALO, groups, LANES).transpose(1, 0, 2)
    b_g = b_dw.reshape(groups, 1, LANES)

    def conv_tile(s):
        return jnp.clip(s - n_cast, 0, n_tiles - 1)

    def proj_tile(s):
        return jnp.maximum(s - n_cast - 1, 0)

    vec = pl.BlockSpec((1, d), lambda s: (0, 0))
    kern = functools.partial(_conv_proj_kernel, n_cast=n_cast, n_tiles=n_tiles, tiles_per_seq=seq // tt,
                             row_chunk=row_chunk, w_layer=w_layer, ln_layer=ln_layer)
    return pl.pallas_call(
        kern,
        grid=(n_cast + n_tiles + 1,),
        in_specs=[
            pl.BlockSpec((tt, d), lambda s: (conv_tile(s), 0)),
            pl.BlockSpec((CONV_HALO, d), lambda s: (jnp.maximum(conv_tile(s) * halo_per_tile - 1, 0), 0)),
            pl.BlockSpec((groups, CONV_HALO, LANES), lambda s: (0, 0, 0)),
            pl.BlockSpec((groups, 1, LANES), lambda s: (0, 0, 0)),
            vec, vec,
            pl.BlockSpec((None, chunk_rows, d), lambda s: (w_layer, jnp.minimum(s, n_cast - 1), 0)),
            pl.BlockSpec(b2.shape, lambda s: (0, 0)),
            pl.BlockSpec((tt, d), lambda s: (proj_tile(s), 0)),
            _mod_spec(gate, lambda s: proj_tile(s) * tt // seq),
            pl.BlockSpec(ln_g.shape, lambda s: (0, 0)),
            pl.BlockSpec(ln_b.shape, lambda s: (0, 0)),
            _mod_spec(sc_next, lambda s: proj_tile(s) * tt // seq),
            _mod_spec(sh_next, lambda s: proj_tile(s) * tt // seq),
            pl.BlockSpec((8, LANES), lambda s: (0, 0)),
        ],
        out_specs=[pl.BlockSpec((tt, d), lambda s: (proj_tile(s), 0))] * 2,
        out_shape=[jax.ShapeDtypeStruct((m, d), F32), jax.ShapeDtypeStruct((m, d), BF16)],
        scratch_shapes=[
            pltpu.VMEM((d // nw, d, nw), BF16),
            pltpu.VMEM((groups, CONV_HALO + tt, LANES), F32),
            pltpu.VMEM((groups, tt, LANES), F32),
            pltpu.VMEM((2, tt, d), BF16),
            pltpu.VMEM((d // nw, tt, nw), F32),
        ],
        compiler_params=_params("arbitrary"),
        name="conv_proj",
    )(u, u, w_g, b_g, cln_g.reshape(1, d), cln_b.reshape(1, d), w2, b2, x, gate[0],
      ln_g, ln_b, sc_next[0], sh_next[0], jnp.zeros((8, LANES), jnp.int32))


def _proj_res_ln_kernel(*refs, has_bias, n_cast, w_layer, ln_layer):
    if has_bias:
        a_ref, w_ref, bias_ref, x_ref, gate_ref, lng_ref, lnb_ref, sc_ref, sh_ref, o_ref, h_ref, w_bf = refs
    else:
        a_ref, w_ref, x_ref, gate_ref, lng_ref, lnb_ref, sc_ref, sh_ref, o_ref, h_ref, w_bf = refs
    s = pl.program_id(0)
    chunk_rows = w_ref.shape[0]

    @pl.when(s < n_cast)
    def _():
        r = pl.multiple_of(s * chunk_rows, chunk_rows)
        w_bf[pl.ds(r, chunk_rows), :] = w_ref[...].astype(BF16)

    @pl.when(s >= n_cast)
    def _():
        w = w_bf[...]
        for r in range(0, a_ref.shape[0], SUB_ROWS):
            rows = slice(r, r + SUB_ROWS)
            y = _dot(a_ref[rows, :], w)
            if has_bias:
                y = y + bias_ref[w_layer:w_layer + 1, :]
            z = DEEPNORM_ALPHA * x_ref[rows, :] + gate_ref[0] * y
            out = _layer_norm(z, lng_ref[ln_layer:ln_layer + 1, :], lnb_ref[ln_layer:ln_layer + 1, :])
            o_ref[rows, :] = out
            h_ref[rows, :] = (out * (1.0 + sc_ref[0]) + sh_ref[0]).astype(h_ref.dtype)


def _proj_res_ln(a, w, bias, x, gate, ln_g, ln_b, sc_next, sh_next, w_layer, ln_layer, seq, tm=512, chunk_rows=256):
    m, k = a.shape
    d = w.shape[2]
    n_cast = k // chunk_rows

    def tile(s):
        return jnp.maximum(s - n_cast, 0)

    in_specs = [
        pl.BlockSpec((tm, k), lambda s: (tile(s), 0)),
        pl.BlockSpec((None, chunk_rows, d), lambda s: (w_layer, jnp.minimum(s, n_cast - 1), 0)),
    ]
    args = [a, w]
    if bias is not None:
        in_specs.append(pl.BlockSpec(bias.shape, lambda s: (0, 0)))
        args.append(bias)
    in_specs += [
        pl.BlockSpec((tm, d), lambda s: (tile(s), 0)),
        _mod_spec(gate, lambda s: tile(s) * tm // seq),
        pl.BlockSpec(ln_g.shape, lambda s: (0, 0)),
        pl.BlockSpec(ln_b.shape, lambda s: (0, 0)),
        _mod_spec(sc_next, lambda s: tile(s) * tm // seq),
        _mod_spec(sh_next, lambda s: tile(s) * tm // seq),
    ]
    args += [x, gate[0], ln_g, ln_b, sc_next[0], sh_next[0]]
    return pl.pallas_call(
        functools.partial(_proj_res_ln_kernel, has_bias=bias is not None, n_cast=n_cast, w_layer=w_layer,
                          ln_layer=ln_layer),
        grid=(n_cast + m // tm,),
        in_specs=in_specs,
        out_specs=[pl.BlockSpec((tm, d), lambda s: (tile(s), 0))] * 2,
        out_shape=[jax.ShapeDtypeStruct((m, d), F32), jax.ShapeDtypeStruct((m, d), BF16)],
        scratch_shapes=[pltpu.VMEM((k, d), BF16)],
        compiler_params=_params("arbitrary"),
        name="proj_res_ln",
    )(*args)


def _mlp_kernel(h_ref, x_hbm, gate_ref, w1_ref, b1_ref, w2_ref, b2_ref, lng_ref, lnb_ref, o_ref, x_vm, x_sem, *,
                layer):
    i = pl.program_id(0)
    f = pl.program_id(1)
    last = pl.num_programs(1) - 1
    tm = o_ref.shape[0]
    x_copy = pltpu.make_async_copy(x_hbm.at[pl.ds(pl.multiple_of(i * tm, tm), tm), :], x_vm, x_sem)

    def chunk(h, w1, w2):
        u = _dot(h, w1) + b1_ref[layer:layer + 1, :]
        return _dot(jnp.square(jnp.maximum(u, 0.0)).astype(BF16), w2)

    @pl.when(f == 0)
    def _():
        x_copy.start()
        o_ref[...] = chunk(h_ref[...], w1_ref[...].astype(BF16), w2_ref[...].astype(BF16))

    @pl.when(jnp.logical_and(f > 0, f < last))
    def _():
        o_ref[...] += chunk(h_ref[...], w1_ref[...].astype(BF16), w2_ref[...].astype(BF16))

    @pl.when(f == last)
    def _():
        x_copy.wait()
        w1, w2 = w1_ref[...].astype(BF16), w2_ref[...].astype(BF16)
        for r in range(0, tm, SUB_ROWS):
            rows = slice(r, r + SUB_ROWS)
            y = o_ref[rows, :] + chunk(h_ref[rows, :], w1, w2) + b2_ref[layer:layer + 1, :]
            z = DEEPNORM_ALPHA * x_vm[rows, :] + gate_ref[0] * y
            o_ref[rows, :] = _layer_norm(z, lng_ref[layer:layer + 1, :], lnb_ref[layer:layer + 1, :])


def _mlp(h, x, gate, w1, b1, w2, b2, ln_g, ln_b, layer, seq, tm=1024, tf=512):
    m, d = x.shape
    dff = w1.shape[2]
    assert dff // tf >= 2
    vec_spec = pl.BlockSpec(b2.shape, lambda i, f: (0, 0))
    return pl.pallas_call(
        functools.partial(_mlp_kernel, layer=layer),
        grid=(m // tm, dff // tf),
        in_specs=[
            pl.BlockSpec((tm, d), lambda i, f: (i, 0)),
            pl.BlockSpec(memory_space=pl.ANY),
            _mod_spec(gate, lambda i, f: i * tm // seq),
            pl.BlockSpec((None, d, tf), lambda i, f: (layer, 0, f)),
            pl.BlockSpec((b1.shape[0], tf), lambda i, f: (0, f)),
            pl.BlockSpec((None, tf, d), lambda i, f: (layer, f, 0)),
            vec_spec, vec_spec, vec_spec,
        ],
        out_specs=pl.BlockSpec((tm, d), lambda i, f: (i, 0)),
        out_shape=jax.ShapeDtypeStruct((m, d), F32),
        scratch_shapes=[pltpu.VMEM((tm, d), F32), pltpu.SemaphoreType.DMA(())],
        compiler_params=_params("arbitrary", "arbitrary"),
        name="mlp",
    )(h, x, gate[0], w1, b1, w2, b2, ln_g, ln_b)


def _qkv_rope_kernel(x_hbm, sc_ref, sh_ref, w_ref, cos_ref, sin_ref, o_ref, w_bf, x_ring, x_sems, *, q_tiles,
                     rope_tiles):
    j = pl.program_id(0)
    x_ref = x_ring.at[_ring_fetch(x_hbm, x_ring, x_sems)]

    @pl.when(pl.program_id(1) == 0)
    def _():
        _cast_rows(w_ref, w_bf)

    tn = o_ref.shape[1]

    def step(rope):
        def run():
            scale = jnp.where(j < q_tiles, DIFF_HEAD_DIM ** -0.5 * LOG2_E, 1.0)
            for r in range(0, x_ref.shape[0], SUB_ROWS):
                rows = slice(r, r + SUB_ROWS)
                h = (x_ref[rows, :] * (1.0 + sc_ref[0]) + sh_ref[0]).astype(BF16)
                y = _dot(h, w_bf[...])
                if not rope:
                    o_ref[rows, :] = y.astype(o_ref.dtype)
                    continue
                cos = cos_ref[rows, :] * scale
                sin = sin_ref[rows, :] * scale
                for c in range(tn // DIFF_HEAD_DIM):
                    lanes = slice(c * DIFF_HEAD_DIM, (c + 1) * DIFF_HEAD_DIM)
                    t = y[:, lanes]
                    o_ref[rows, lanes] = (t * cos + pltpu.roll(t, DIFF_HEAD_DIM // 2, 1) * sin).astype(o_ref.dtype)

        return run

    pl.when(j < rope_tiles)(step(True))
    pl.when(j >= rope_tiles)(step(False))


def _qkv_rope(x, sc, sh, w, cos, sin, layer, seq, tm=1024, tn=1024):
    m, d = x.shape
    n = w.shape[2]
    kern = functools.partial(_qkv_rope_kernel, q_tiles=D_MODEL // tn, rope_tiles=2 * D_MODEL // tn)
    return pl.pallas_call(
        kern,
        grid=(n // tn, m // tm),
        in_specs=[
            pl.BlockSpec(memory_space=pl.ANY),
            _mod_spec(sc, lambda j, i: i * tm // seq),
            _mod_spec(sh, lambda j, i: i * tm // seq),
            pl.BlockSpec((None, d, tn), lambda j, i: (layer, 0, j)),
            pl.BlockSpec((tm, DIFF_HEAD_DIM), lambda j, i: (i, 0)),
            pl.BlockSpec((tm, DIFF_HEAD_DIM), lambda j, i: (i, 0)),
        ],
        out_specs=pl.BlockSpec((tm, tn), lambda j, i: (i, j)),
        out_shape=jax.ShapeDtypeStruct((m, n), BF16),
        scratch_shapes=[pltpu.VMEM((d, tn), BF16), pltpu.VMEM((RING_SLOTS, tm, d), F32),
                        pltpu.SemaphoreType.DMA((RING_SLOTS,))],
        compiler_params=_params("arbitrary", "arbitrary"),
        name="qkv_rope",
    )(x, sc[0], sh[0], w, cos, sin)


def _diff_attn_kernel(q_ref, k_ref, v_ref, lq1_ref, lk1_ref, lq2_ref, lk2_ref, g_ref, o_ref, *, lambda_init, layer):
    tq = q_ref.shape[0]
    dh = DIFF_HEAD_DIM
    vd = DIFF_V_DIM
    layer_row = slice(layer, layer + 1)
    lam = (jnp.exp(jnp.sum(lq1_ref[layer_row, :] * lk1_ref[layer_row, :], axis=-1, keepdims=True))
           - jnp.exp(jnp.sum(lq2_ref[layer_row, :] * lk2_ref[layer_row, :], axis=-1, keepdims=True)) + lambda_init)
    subln_g = g_ref[layer_row, :]

    def head_stages(e, n_chunks, row, col):
        st = {"s": [[None] * n_chunks, [None] * n_chunks], "p": [[None] * n_chunks, [None] * n_chunks],
              "mx": [None, None], "sum": [None, None], "coef": [None, None], "acc": None}

        def scores(c):
            def step():
                for mp in range(2):
                    lanes = slice(e * vd + mp * dh, e * vd + (mp + 1) * dh)
                    s = lax.dot_general(q_ref[:, lanes], k_ref[c * tq:(c + 1) * tq, lanes], (((1,), (1,)), ((), ())),
                                        preferred_element_type=F32)
                    if c == n_chunks - 1:
                        s = jnp.where(row >= col, s, -jnp.inf)
                    st["s"][mp][c] = s
                    part = jnp.maximum(s[:, :LANES], s[:, LANES:])
                    st["mx"][mp] = part if c == 0 else jnp.maximum(st["mx"][mp], part)
            return step

        def row_max():
            for mp in range(2):
                st["mx"][mp] = jnp.max(st["mx"][mp], axis=-1, keepdims=True)

        def exps(c):
            def step():
                for mp in range(2):
                    p = jnp.exp2(st["s"][mp][c] - st["mx"][mp])
                    st["p"][mp][c] = p
                    part = p[:, :LANES] + p[:, LANES:]
                    st["sum"][mp] = part if c == 0 else st["sum"][mp] + part
            return step

        def coefs():
            l1, l2 = [jnp.sum(st["sum"][mp], axis=-1, keepdims=True) for mp in range(2)]
            st["coef"] = [1.0 / l1, lam * l1 / l2]

        def values(c):
            def step():
                a = st["p"][0][c] - st["p"][1][c] * st["coef"][1]
                o = _dot(a.astype(v_ref.dtype), v_ref[c * tq:(c + 1) * tq, e * vd:(e + 1) * vd])
                st["acc"] = o if c == 0 else st["acc"] + o
            return step

        def finish():
            o = st["acc"] * st["coef"][0]
            o = o * lax.rsqrt(jnp.mean(o * o, axis=-1, keepdims=True) + RMS_EPS)
            o = o * subln_g * (1.0 - lambda_init)
            o_ref[:, e * vd:(e + 1) * vd] = o.astype(o_ref.dtype)

        chunks = range(n_chunks)
        return ([scores(c) for c in chunks],
                [row_max] + [exps(c) for c in chunks] + [coefs],
                [values(c) for c in chunks] + [finish])

    def tile(n_chunks):
        def run():
            row = lax.broadcasted_iota(jnp.int32, (tq, tq), 0)
            col = lax.broadcasted_iota(jnp.int32, (tq, tq), 1)
            heads = [head_stages(e, n_chunks, row, col) for e in range(q_ref.shape[1] // vd)]
            for t in range(len(heads) + 2):
                live = [heads[t - k][k] for k in range(3) if 0 <= t - k < len(heads)]
                for j in range(max(len(steps) for steps in live)):
                    for steps in live:
                        if j < len(steps):
                            steps[j]()

        return run

    i = pl.program_id(2)
    for n_chunks in range(1, k_ref.shape[0] // tq + 1):
        pl.when(i == n_chunks - 1)(tile(n_chunks))


def _diff_attn(qkv, lq1, lk1, lq2, lk2, subln_g, layer, bsz, seq, lambda_init, tq=256, heads_per_step=4):
    m = qkv.shape[0]
    nq = seq // tq
    vd = DIFF_V_DIM
    gw = heads_per_step * vd
    groups = DIFF_HEADS // heads_per_step
    lam_spec = pl.BlockSpec(lq1.shape, lambda b, h, i: (0, 0))
    return pl.pallas_call(
        functools.partial(_diff_attn_kernel, lambda_init=lambda_init, layer=layer),
        grid=(bsz, groups, nq),
        in_specs=[
            pl.BlockSpec((tq, gw), lambda b, h, i: (b * nq + i, h)),
            pl.BlockSpec((seq, gw), lambda b, h, i: (b, groups + h)),
            pl.BlockSpec((seq, gw), lambda b, h, i: (b, 2 * groups + h)),
            lam_spec, lam_spec, lam_spec, lam_spec,
            pl.BlockSpec(subln_g.shape, lambda b, h, i: (0, 0)),
        ],
        out_specs=pl.BlockSpec((tq, gw), lambda b, h, i: (b * nq + i, h)),
        out_shape=jax.ShapeDtypeStruct((m, DIFF_HEADS * vd), BF16),
        compiler_params=_params("arbitrary", "arbitrary", "arbitrary"),
        name="diff_attn",
    )(qkv, qkv, qkv, lq1, lk1, lq2, lk2, subln_g)


def kernel(x, c, positions, ada_w, ada_b, ln_mix_g, ln_mix_b, ln_ffn_g, ln_ffn_b, conv_pw1_w, conv_pw1_b, conv_dw_w, conv_dw_b, conv_ln_g, conv_ln_b, conv_pw2_w, conv_pw2_b, attn_qkv_w, attn_lq1, attn_lk1, attn_lq2, attn_lk2, attn_subln_g, attn_o_w, mlp_w1, mlp_b1, mlp_w2, mlp_b2):
    bsz, seq, d = x.shape
    xf = x.reshape(bsz * seq, d)
    mod, cos, sin = _adaln_rope(c, ada_w, ada_b, positions)
    mod = mod.reshape(DEPTH, bsz, 6, 1, d)
    for i in range(DEPTH):
        sh_m, sc_m, g_m, sh_f, sc_f, g_f = [(mod, i, k) for k in range(6)]
        j = i // N_MIXERS
        if i % N_MIXERS == 0:
            u = _pw1_glu(xf, sc_m, sh_m, conv_pw1_w, conv_pw1_b, j, seq)
            xf, hf = _conv_proj(u, conv_dw_w[j], conv_dw_b[j], conv_ln_g[j], conv_ln_b[j], conv_pw2_w, conv_pw2_b,
                                xf, g_m, ln_mix_g, ln_mix_b, sc_f, sh_f, j, i, seq)
        else:
            lambda_init = 0.8 - 0.6 * math.exp(-0.3 * i)
            qkv = _qkv_rope(xf, sc_m, sh_m, attn_qkv_w, cos, sin, j, seq)
            o = _diff_attn(qkv, attn_lq1, attn_lk1, attn_lq2, attn_lk2, attn_subln_g, j, bsz, seq, lambda_init)
            xf, hf = _proj_res_ln(o, attn_o_w, None, xf, g_m, ln_mix_g, ln_mix_b, sc_f, sh_f, j, i, seq)
        xf = _mlp(hf, xf, g_f, mlp_w1, mlp_b1, mlp_w2, mlp_b2, ln_ffn_g, ln_ffn_b, i, seq)
    return xf.reshape(bsz, seq, d)
```
